```python
import math
import jax, jax.numpy as jnp
from jax import lax
import numpy as np

D_MODEL = 1024
BATCH = 2
SEQ = 8192
DEPTH = 2

EXPAND = 2
D_INNER = EXPAND * D_MODEL
S5_WIDTH = D_INNER // 2
S5_GROUP = 16
S5_GROUPS = S5_WIDTH // S5_GROUP
S5_STATE = 64
RET_WIDTH = D_INNER - S5_WIDTH
RET_HEADS = 4
RET_DK = RET_WIDTH // RET_HEADS
RET_DV = RET_WIDTH // RET_HEADS
RET_CHUNK = 128
ROPE_BASE = 10000.0
EVEN_SPLITS = [S5_WIDTH, S5_WIDTH, RET_HEADS * RET_DK, RET_HEADS * RET_DK, RET_HEADS * RET_DV, RET_HEADS * RET_DV]
EVEN_IN = sum(EVEN_SPLITS)
SGU_WIDTH = D_INNER
SGU_GROUPS = 4
SGU_GROUP_DIM = SGU_WIDTH // SGU_GROUPS
SGU_CHUNK = 128
ODD_IN = 3 * SGU_WIDTH
N_EVEN = (DEPTH + 1) // 2
N_ODD = DEPTH // 2
NORM_EPS = 1e-6

kernel_name = "hybrid_s5_retention_sgu_block"

F32 = jnp.float32


def rms_norm(x, g):
    xf = x.astype(F32)
    y = xf * lax.rsqrt(jnp.mean(xf * xf, axis=-1, keepdims=True) + NORM_EPS)
    return (y * g.astype(F32)).astype(x.dtype)


def rotary(x, pos):
    half = x.shape[-1] // 2
    inv = ROPE_BASE ** (-jnp.arange(half, dtype=F32) / half)
    ang = pos[:, None] * inv[None, :]
    cos = jnp.cos(ang)[None, :, None, :]
    sin = jnp.sin(ang)[None, :, None, :]
    x1, x2 = x[..., :half], x[..., half:]
    return jnp.concatenate([x1 * cos - x2 * sin, x1 * sin + x2 * cos], axis=-1)


def s5_branch(u, lam_re, lam_im, log_dt, b_re, b_im, c_re, c_im, d_skip, w_glu, b_glu):
    bsz, seq, _ = u.shape
    uf = u.astype(F32).reshape(bsz, seq, S5_GROUPS, S5_GROUP)
    lr = jnp.minimum(lam_re.astype(F32), -1e-4)
    li = lam_im.astype(F32)
    dt = jnp.exp(log_dt.astype(F32))[:, None]
    mag = jnp.exp(lr * dt)
    ab_re = mag * jnp.cos(li * dt)
    ab_im = mag * jnp.sin(li * dt)
    den = lr * lr + li * li
    n_re = ab_re - 1.0
    n_im = ab_im
    z_re = (n_re * lr + n_im * li) / den
    z_im = (n_im * lr - n_re * li) / den
    br = b_re.astype(F32)
    bi = b_im.astype(F32)
    bb_re = z_re[..., None] * br - z_im[..., None] * bi
    bb_im = z_re[..., None] * bi + z_im[..., None] * br
    bu_re = jnp.einsum('gph,blgh->blgp', bb_re, uf)
    bu_im = jnp.einsum('gph,blgh->blgp', bb_im, uf)
    a_re = jnp.broadcast_to(ab_re, bu_re.shape)
    a_im = jnp.broadcast_to(ab_im, bu_im.shape)

    def combine(left, right):
        a1r, a1i, b1r, b1i = left
        a2r, a2i, b2r, b2i = right
        return (a2r * a1r - a2i * a1i,
                a2r * a1i + a2i * a1r,
                a2r * b1r - a2i * b1i + b2r,
                a2r * b1i + a2i * b1r + b2i)

    _, _, s_re, s_im = lax.associative_scan(combine, (a_re, a_im, bu_re, bu_im), axis=1)
    y = (jnp.einsum('ghp,blgp->blgh', c_re.astype(F32), s_re)
         - jnp.einsum('ghp,blgp->blgh', c_im.astype(F32), s_im))
    y = y + d_skip.astype(F32).reshape(S5_GROUPS, S5_GROUP) * uf
    y = jax.nn.gelu(y.reshape(bsz, seq, S5_WIDTH))
    y = y * jax.nn.sigmoid(y @ w_glu.astype(F32) + b_glu.astype(F32))
    return y.astype(u.dtype)


def retention_branch(q, k, v, gn_gain):
    bsz, seq, _ = q.shape
    nc = seq // RET_CHUNK
    pos = jnp.arange(seq, dtype=F32)
    qh = rotary(q.astype(F32).reshape(bsz, seq, RET_HEADS, RET_DK), pos)
    kh = rotary(k.astype(F32).reshape(bsz, seq, RET_HEADS, RET_DK), pos) * (RET_DK ** -0.5)
    vh = v.astype(F32).reshape(bsz, seq, RET_HEADS, RET_DV)
    log_g = jnp.log1p(-jnp.exp2(-5.0 - jnp.arange(RET_HEADS, dtype=F32)))
    idx = jnp.arange(RET_CHUNK, dtype=F32)
    diff = idx[:, None] - idx[None, :]
    decay = jnp.where(diff >= 0, jnp.exp(log_g[:, None, None] * jnp.maximum(diff, 0.0)), 0.0)
    xi = jnp.exp(log_g[None, :] * (idx[:, None] + 1.0))
    zeta = jnp.exp(log_g[None, :] * (RET_CHUNK - 1.0 - idx[:, None]))
    chunk_decay = jnp.exp(log_g * RET_CHUNK)
    qc = qh.reshape(bsz, nc, RET_CHUNK, RET_HEADS, RET_DK)
    kc = kh.reshape(bsz, nc, RET_CHUNK, RET_HEADS, RET_DK)
    vc = vh.reshape(bsz, nc, RET_CHUNK, RET_HEADS, RET_DV)
    scores = jnp.einsum('bcnhk,bcmhk->bchnm', qc, kc) * decay[None, None]
    inner = jnp.einsum('bchnm,bcmhv->bcnhv', scores, vc)
    local = jnp.einsum('bcmhk,bcmhv->bchkv', kc * zeta[None, None, :, :, None], vc)

    def step(state, s_chunk):
        return state * chunk_decay[None, :, None, None] + s_chunk, state

    init = jnp.zeros((bsz, RET_HEADS, RET_DK, RET_DV), F32)
    _, prev = lax.scan(step, init, jnp.moveaxis(local, 1, 0))
    prev = jnp.moveaxis(prev, 0, 1)
    cross = jnp.einsum('bcnhk,bchkv->bcnhv', qc * xi[None, None, :, :, None], prev)
    o = (inner + cross).reshape(bsz, seq, RET_HEADS, RET_DV)
    mu = jnp.mean(o, axis=-1, keepdims=True)
    var = jnp.mean(jnp.square(o - mu), axis=-1, keepdims=True)
    o = (o - mu) * lax.rsqrt(var + NORM_EPS)
    o = o.reshape(bsz, seq, RET_HEADS * RET_DV) * gn_gain.astype(F32)
    return o.astype(q.dtype)


def spatial_gating_branch(h, v_gain, w_s, b_s):
    bsz, seq, _ = h.shape
    nc = seq // SGU_CHUNK
    u, v = h[..., :SGU_WIDTH], h[..., SGU_WIDTH:]
    vf = v.astype(F32)
    mu = jnp.mean(vf, axis=-1, keepdims=True)
    var = jnp.mean(jnp.square(vf - mu), axis=-1, keepdims=True)
    vf = (vf - mu) * lax.rsqrt(var + NORM_EPS) * v_gain.astype(F32)
    vc = vf.reshape(bsz, nc, SGU_CHUNK, SGU_GROUPS, SGU_GROUP_DIM)
    mask = jnp.tril(jnp.ones((SGU_CHUNK, SGU_CHUNK), dtype=bool))
    w = jnp.where(mask[None], w_s.astype(F32), 0.0)
    s = jnp.einsum('gts,bcsgd->bctgd', w, vc) + b_s.astype(F32).T[None, None, :, :, None]
    return (u.astype(F32) * s.reshape(bsz, seq, SGU_WIDTH)).astype(h.dtype)


def setup_inputs(seed: int = 0) -> dict:
    key = jax.random.key(seed)
    ks = jax.random.split(key, 24)
    nrm = lambda k, shape, scale: jax.random.normal(k, shape, F32) * scale
    x = jax.random.normal(ks[0], (BATCH, SEQ, D_MODEL), F32)
    norm_even = 1.0 + nrm(ks[1], (N_EVEN, D_MODEL), 0.02)
    w_in_even = nrm(ks[2], (N_EVEN, D_MODEL, EVEN_IN), D_MODEL ** -0.5)
    s5_lam_re = -0.5 + nrm(ks[3], (N_EVEN, S5_GROUPS, S5_STATE), 0.01)
    s5_lam_im = math.pi * jnp.arange(S5_STATE, dtype=F32)[None, None, :] + nrm(ks[4], (N_EVEN, S5_GROUPS, S5_STATE), 0.01)
    s5_log_dt = jax.random.uniform(ks[5], (N_EVEN, S5_GROUPS), F32, math.log(0.001), math.log(0.1))
    s5_b_re = nrm(ks[6], (N_EVEN, S5_GROUPS, S5_STATE, S5_GROUP), (2 * S5_GROUP) ** -0.5)
    s5_b_im = nrm(ks[7], (N_EVEN, S5_GROUPS, S5_STATE, S5_GROUP), (2 * S5_GROUP) ** -0.5)
    s5_c_re = nrm(ks[8], (N_EVEN, S5_GROUPS, S5_GROUP, S5_STATE), (2 * S5_STATE) ** -0.5)
    s5_c_im = nrm(ks[9], (N_EVEN, S5_GROUPS, S5_GROUP, S5_STATE), (2 * S5_STATE) ** -0.5)
    s5_d = nrm(ks[10], (N_EVEN, S5_WIDTH), 1.0)
    s5_w_glu = nrm(ks[11], (N_EVEN, S5_WIDTH, S5_WIDTH), S5_WIDTH ** -0.5)
    s5_b_glu = nrm(ks[12], (N_EVEN, S5_WIDTH), 0.01)
    ret_gn_gain = 1.0 + nrm(ks[13], (N_EVEN, RET_HEADS * RET_DV), 0.02)
    w_out_even = nrm(ks[14], (N_EVEN, D_INNER, D_MODEL), D_INNER ** -0.5)
    norm_odd = 1.0 + nrm(ks[15], (N_ODD, D_MODEL), 0.02)
    w_in_odd = nrm(ks[16], (N_ODD, D_MODEL, ODD_IN), D_MODEL ** -0.5)
    sgu_norm_gain = 1.0 + nrm(ks[17], (N_ODD, SGU_WIDTH), 0.02)
    sgu_w_spatial = nrm(ks[18], (N_ODD, SGU_GROUPS, SGU_CHUNK, SGU_CHUNK), SGU_CHUNK ** -0.5)
    sgu_b_spatial = 1.0 + nrm(ks[19], (N_ODD, SGU_GROUPS, SGU_CHUNK), 0.02)
    w_out_odd = nrm(ks[20], (N_ODD, SGU_WIDTH, D_MODEL), SGU_WIDTH ** -0.5)
    final_norm = 1.0 + nrm(ks[21], (D_MODEL,), 0.02)
    return {"x": x, "norm_even": norm_even, "w_in_even": w_in_even,
            "s5_lam_re": s5_lam_re, "s5_lam_im": s5_lam_im, "s5_log_dt": s5_log_dt,
            "s5_b_re": s5_b_re, "s5_b_im": s5_b_im, "s5_c_re": s5_c_re, "s5_c_im": s5_c_im,
            "s5_d": s5_d, "s5_w_glu": s5_w_glu, "s5_b_glu": s5_b_glu,
            "ret_gn_gain": ret_gn_gain, "w_out_even": w_out_even,
            "norm_odd": norm_odd, "w_in_odd": w_in_odd, "sgu_norm_gain": sgu_norm_gain,
            "sgu_w_spatial": sgu_w_spatial, "sgu_b_spatial": sgu_b_spatial,
            "w_out_odd": w_out_odd, "final_norm": final_norm}


def reference(x, norm_even, w_in_even, s5_lam_re, s5_lam_im, s5_log_dt, s5_b_re, s5_b_im,
              s5_c_re, s5_c_im, s5_d, s5_w_glu, s5_b_glu, ret_gn_gain, w_out_even,
              norm_odd, w_in_odd, sgu_norm_gain, sgu_w_spatial, sgu_b_spatial, w_out_odd,
              final_norm):
    split_pts = [int(p) for p in np.cumsum(EVEN_SPLITS)[:-1]]
    for layer in range(DEPTH):
        i = layer // 2
        if layer % 2 == 0:
            h = rms_norm(x, norm_even[i])
            p = h @ w_in_even[i]
            a_u, a_z, q, k, v, b_z = jnp.split(p, split_pts, axis=-1)
            ya = s5_branch(a_u, s5_lam_re[i], s5_lam_im[i], s5_log_dt[i], s5_b_re[i], s5_b_im[i],
                           s5_c_re[i], s5_c_im[i], s5_d[i], s5_w_glu[i], s5_b_glu[i]) * jax.nn.silu(a_z)
            yb = retention_branch(q, k, v, ret_gn_gain[i]) * jax.nn.silu(b_z)
            x = x + jnp.concatenate([ya, yb], axis=-1) @ w_out_even[i]
        else:
            h = rms_norm(x, norm_odd[i])
            p = h @ w_in_odd[i]
            hz = jax.nn.gelu(p[..., :2 * SGU_WIDTH])
            z = p[..., 2 * SGU_WIDTH:]
            y = spatial_gating_branch(hz, sgu_norm_gain[i], sgu_w_spatial[i], sgu_b_spatial[i]) * jax.nn.silu(z)
            x = x + y @ w_out_odd[i]
    return rms_norm(x, final_norm)
```

```python
import functools
import math

import jax
import jax.numpy as jnp
import numpy as np
from jax import lax
from jax.experimental import pallas as pl
from jax.experimental.pallas import tpu as pltpu

F32 = jnp.float32
BF16 = jnp.bfloat16

D_MODEL = 1024
S5_WIDTH = 1024
S5_GROUP = 16
S5_GROUPS = 64
S5_STATE = 64
RET_HEADS = 4
RET_DK = 256
RET_DV = 256
ROPE_BASE = 10000.0
SGU_WIDTH = 2048
SGU_GROUPS = 4
SGU_GROUP_DIM = 512
SGU_CHUNK = 128
NORM_EPS = 1e-6

V7X_VMEM_BYTES = 64 * 1024 * 1024
VMEM_LIMIT_BYTES = V7X_VMEM_BYTES - 12 * 1024 * 1024
SUBLANES = 8
LANES = 128

PROJ_TILE = 512
S5_TILE = 512
S5_SEGS = SUBLANES
S5_SEG_LEN = S5_TILE // S5_SEGS
S5_KT = 4
S5_KT_IN = S5_WIDTH // S5_KT
S5_KT_STATE = S5_GROUPS // S5_KT * S5_STATE
S5_SCAN_LANES = 512
RET_TILE = 256
ODD_TILE = 512


def _const_spec(shape):
    zeros = (0,) * len(shape)
    return pl.BlockSpec(shape, lambda *_: zeros, pipeline_mode=pl.Buffered(1))


def _rms_norm(x, gain):
    ms = jnp.mean(x * x, axis=-1, keepdims=True)
    return x * lax.rsqrt(ms + NORM_EPS) * gain


def _silu(x):
    return x * jax.nn.sigmoid(x)


def _even_in_kernel(x_ref, g_ref, w_ref, cos_ref, sin_ref,
                    u_ref, az_ref, q_ref, k_ref, v_ref, bz_ref):
    hn = _rms_norm(x_ref[...], g_ref[...]).astype(BF16)

    def proj(j):
        return jnp.dot(hn, w_ref[:, j * D_MODEL:(j + 1) * D_MODEL], preferred_element_type=F32)

    def rotary_store(p, out_ref, scale):
        cos = cos_ref[...]
        sin = sin_ref[...]
        half = RET_DK // 2
        for h in range(RET_HEADS):
            x1 = p[:, h * RET_DK:h * RET_DK + half]
            x2 = p[:, h * RET_DK + half:(h + 1) * RET_DK]
            o1 = x1 * cos - x2 * sin
            o2 = x1 * sin + x2 * cos
            if scale != 1.0:
                o1 = o1 * scale
                o2 = o2 * scale
            out_ref[:, h * RET_DK:h * RET_DK + half] = o1.astype(BF16)
            out_ref[:, h * RET_DK + half:(h + 1) * RET_DK] = o2.astype(BF16)

    u_ref[...] = proj(0).astype(BF16)
    az_ref[...] = _silu(proj(1)).astype(BF16)
    rotary_store(proj(2), q_ref, 1.0)
    rotary_store(proj(3), k_ref, RET_DK ** -0.5)
    v_ref[...] = proj(4).astype(BF16)
    bz_ref[...] = _silu(proj(5)).astype(BF16)


def _even_in_proj(x2d, gain, w_in, cos, sin, seq):
    n = x2d.shape[0]
    tm = PROJ_TILE
    tiles_per_seq = seq // tm
    tok = pl.BlockSpec((tm, D_MODEL), lambda i: (i, 0))
    rot = pl.BlockSpec((tm, RET_DK // 2), lambda i: (i % tiles_per_seq, 0))
    out = jax.ShapeDtypeStruct((n, D_MODEL), BF16)
    return pl.pallas_call(
        _even_in_kernel,
        grid=(n // tm,),
        in_specs=[tok, _const_spec((1, D_MODEL)), _const_spec(w_in.shape), rot, rot],
        out_specs=[tok] * 6,
        out_shape=[out] * 6,
        compiler_params=pltpu.CompilerParams(
            dimension_semantics=("arbitrary",), vmem_limit_bytes=VMEM_LIMIT_BYTES),
        name="even_in_proj",
    )(x2d, gain, w_in, cos, sin)


def _s5_kernel(u_ref, az_ref, perm_ref, permt_ref, bmat_ref, cmat_ref, a_ref, d_ref,
               wglu_ref, bglu_ref, ya_ref,
               sbuf, uperm, yperm, ebuf, sinbuf, carry):
    t = pl.program_id(1)

    @pl.when(t == 0)
    def _():
        carry[...] = jnp.zeros_like(carry)

    uperm[...] = jnp.dot(perm_ref[...], u_ref[...], preferred_element_type=F32).astype(BF16)

    n_half = S5_KT_STATE // S5_SCAN_LANES
    for kt in range(S5_KT):
        sbuf[...] = jnp.dot(uperm[:, kt * S5_KT_IN:(kt + 1) * S5_KT_IN], bmat_ref[kt],
                            preferred_element_type=F32)

        def scan(half, init_r, init_i, store):
            lo = half * S5_SCAN_LANES
            re = slice(lo, lo + S5_SCAN_LANES)
            im = slice(S5_KT_STATE + lo, S5_KT_STATE + lo + S5_SCAN_LANES)
            ar = jnp.broadcast_to(a_ref[kt, 0:1, re], (SUBLANES, S5_SCAN_LANES))
            ai = jnp.broadcast_to(a_ref[kt, 1:2, re], (SUBLANES, S5_SCAN_LANES))

            def step(i, state):
                sr, si = state
                rows = pl.ds(pl.multiple_of(i * SUBLANES, SUBLANES), SUBLANES)
                nr = ar * sr - ai * si + sbuf[rows, re]
                ni = ar * si + ai * sr + sbuf[rows, im]
                if store:
                    sbuf[rows, re] = nr
                    sbuf[rows, im] = ni
                return nr, ni

            return lax.fori_loop(0, S5_SEG_LEN, step, (init_r, init_i), unroll=8)

        zero = jnp.zeros((SUBLANES, S5_SCAN_LANES), F32)
        for half in range(n_half):
            lo = half * S5_SCAN_LANES
            er, ei = scan(half, zero, zero, False)
            ebuf[:, lo:lo + S5_SCAN_LANES] = er
            ebuf[:, S5_KT_STATE + lo:S5_KT_STATE + lo + S5_SCAN_LANES] = ei

        cr = carry[kt, :, 0:S5_KT_STATE]
        ci = carry[kt, :, S5_KT_STATE:]
        atr = a_ref[kt, 2:3, :]
        ati = a_ref[kt, 3:4, :]
        for seg in range(S5_SEGS):
            sinbuf[seg:seg + 1, 0:S5_KT_STATE] = cr
            sinbuf[seg:seg + 1, S5_KT_STATE:] = ci
            er = ebuf[seg:seg + 1, 0:S5_KT_STATE]
            ei = ebuf[seg:seg + 1, S5_KT_STATE:]
            cr, ci = atr * cr - ati * ci + er, atr * ci + ati * cr + ei
        carry[kt, :, 0:S5_KT_STATE] = cr
        carry[kt, :, S5_KT_STATE:] = ci

        for half in range(n_half):
            lo = half * S5_SCAN_LANES
            scan(half, sinbuf[:, lo:lo + S5_SCAN_LANES],
                 sinbuf[:, S5_KT_STATE + lo:S5_KT_STATE + lo + S5_SCAN_LANES], True)

        yperm[:, kt * S5_KT_IN:(kt + 1) * S5_KT_IN] = jnp.dot(
            sbuf[...].astype(BF16), cmat_ref[kt], preferred_element_type=F32)

    yp = yperm[...]
    hi = yp.astype(BF16)
    lo_part = (yp - hi.astype(F32)).astype(BF16)
    y = (jnp.dot(permt_ref[...], hi, preferred_element_type=F32)
         + jnp.dot(permt_ref[...], lo_part, preferred_element_type=F32))
    y = y + d_ref[...] * u_ref[...].astype(F32)
    y = jax.nn.gelu(y)
    gl = jnp.dot(y.astype(BF16), wglu_ref[...], preferred_element_type=F32) + bglu_ref[...]
    y = y * jax.nn.sigmoid(gl)
    ya_ref[...] = (y * az_ref[...].astype(F32)).astype(BF16)


def _s5_perm_matrix():
    p = np.zeros((S5_TILE, S5_TILE), np.float32)
    for seg in range(S5_SEGS):
        for i in range(S5_SEG_LEN):
            p[i * S5_SEGS + seg, seg * S5_SEG_LEN + i] = 1.0
    return p


def _s5_params(lam_re, lam_im, log_dt, b_re, b_im, c_re, c_im):
    lr = jnp.minimum(lam_re.astype(F32), -1e-4)
    li = lam_im.astype(F32)
    dt = jnp.exp(log_dt.astype(F32))[:, None]
    mag = jnp.exp(lr * dt)
    ab_re = mag * jnp.cos(li * dt)
    ab_im = mag * jnp.sin(li * dt)
    den = lr * lr + li * li
    n_re = ab_re - 1.0
    n_im = ab_im
    z_re = (n_re * lr + n_im * li) / den
    z_im = (n_im * lr - n_re * li) / den
    br = b_re.astype(F32)
    bi = b_im.astype(F32)
    bb_re = z_re[..., None] * br - z_im[..., None] * bi
    bb_im = z_re[..., None] * bi + z_im[..., None] * br
    magt = jnp.exp(lr * dt * S5_SEG_LEN)
    at_re = magt * jnp.cos(li * dt * S5_SEG_LEN)
    at_im = magt * jnp.sin(li * dt * S5_SEG_LEN)
    a_tab = jnp.stack([x.reshape(S5_KT, S5_KT_STATE) for x in (ab_re, ab_im, at_re, at_im)], axis=1)

    gl = S5_GROUPS // S5_KT
    eye = jnp.eye(gl, dtype=F32)

    def b_block(bb):
        x = jnp.transpose(bb, (0, 2, 1)).reshape(S5_KT, gl, S5_GROUP, S5_STATE)
        return jnp.einsum('kghp,gj->kghjp', x, eye).reshape(S5_KT, S5_KT_IN, S5_KT_STATE)

    def c_block(c):
        x = jnp.transpose(c, (0, 2, 1)).reshape(S5_KT, gl, S5_STATE, S5_GROUP)
        return jnp.einsum('kgph,gj->kgpjh', x, eye).reshape(S5_KT, S5_KT_STATE, S5_KT_IN)

    bmat = jnp.concatenate([b_block(bb_re), b_block(bb_im)], axis=2).astype(BF16)
    cmat = jnp.concatenate([c_block(c_re.astype(F32)), c_block(-c_im.astype(F32))], axis=1).astype(BF16)
    return a_tab, bmat, cmat


def _s5_branch(u, az, a_tab, bmat, cmat, d_skip, w_glu, b_glu, bsz, seq):
    tiles = seq // S5_TILE
    perm = _s5_perm_matrix()
    tok = pl.BlockSpec((S5_TILE, D_MODEL), lambda b, t: (b * tiles + t, 0))
    return pl.pallas_call(
        _s5_kernel,
        grid=(bsz, tiles),
        in_specs=[tok, tok,
                  _const_spec((S5_TILE, S5_TILE)), _const_spec((S5_TILE, S5_TILE)),
                  _const_spec(bmat.shape), _const_spec(cmat.shape), _const_spec(a_tab.shape),
                  _const_spec((1, S5_WIDTH)), _const_spec(w_glu.shape), _const_spec((1, S5_WIDTH))],
        out_specs=tok,
        out_shape=jax.ShapeDtypeStruct(u.shape, BF16),
        scratch_shapes=[
            pltpu.VMEM((S5_TILE, 2 * S5_KT_STATE), F32),
            pltpu.VMEM((S5_TILE, S5_WIDTH), BF16),
            pltpu.VMEM((S5_TILE, S5_WIDTH), F32),
            pltpu.VMEM((S5_SEGS, 2 * S5_KT_STATE), F32),
            pltpu.VMEM((S5_SEGS, 2 * S5_KT_STATE), F32),
            pltpu.VMEM((S5_KT, 1, 2 * S5_KT_STATE), F32),
        ],
        compiler_params=pltpu.CompilerParams(
            dimension_semantics=("arbitrary", "arbitrary"), vmem_limit_bytes=VMEM_LIMIT_BYTES),
        name="s5_branch",
    )(u, az, jnp.asarray(perm, BF16), jnp.asarray(perm.T, BF16), bmat, cmat, a_tab,
      d_skip, w_glu, b_glu)


def _retention_kernel(q_ref, k_ref, v_ref, bz_ref, gain_ref, decay_ref, xi_ref, zeta_ref,
                      yb_ref, state, *, chunk_decay):
    c = pl.program_id(1)

    @pl.when(c == 0)
    def _():
        state[...] = jnp.zeros_like(state)

    for h in range(RET_HEADS):
        cols = slice(h * RET_DK, (h + 1) * RET_DK)
        qh = q_ref[:, cols]
        kh = k_ref[:, cols]
        vh = v_ref[:, cols]
        scores = lax.dot_general(qh, kh, (((1,), (1,)), ((), ())), preferred_element_type=F32)
        scores = scores * decay_ref[h]
        inner = jnp.dot(scores.astype(BF16), vh, preferred_element_type=F32)
        prev = state[h]
        qx = (qh.astype(F32) * xi_ref[:, h:h + 1]).astype(BF16)
        cross = jnp.dot(qx, prev.astype(BF16), preferred_element_type=F32)
        kz = (kh.astype(F32) * zeta_ref[:, h:h + 1]).astype(BF16)
        local = lax.dot_general(kz, vh, (((0,), (0,)), ((), ())), preferred_element_type=F32)
        state[h] = prev * chunk_decay[h] + local
        o = inner + cross
        mu = jnp.mean(o, axis=-1, keepdims=True)
        oc = o - mu
        var = jnp.mean(oc * oc, axis=-1, keepdims=True)
        o = oc * lax.rsqrt(var + NORM_EPS) * gain_ref[:, cols]
        yb_ref[:, cols] = (o * bz_ref[:, cols].astype(F32)).astype(BF16)


def _retention_tables():
    log_g = np.log1p(-np.exp2(-5.0 - np.arange(RET_HEADS, dtype=np.float64)))
    idx = np.arange(RET_TILE, dtype=np.float64)
    diff = idx[:, None] - idx[None, :]
    decay = np.where(diff >= 0, np.exp(log_g[:, None, None] * np.maximum(diff, 0.0)), 0.0)
    xi = np.exp(log_g[None, :] * (idx[:, None] + 1.0))
    zeta = np.exp(log_g[None, :] * (RET_TILE - 1.0 - idx[:, None]))
    chunk_decay = tuple(float(x) for x in np.exp(log_g * RET_TILE))
    return (jnp.asarray(decay, F32), jnp.asarray(xi, F32), jnp.asarray(zeta, F32), chunk_decay)


def _retention_branch(q, k, v, bz, gain, bsz, seq):
    chunks = seq // RET_TILE
    decay, xi, zeta, chunk_decay = _retention_tables()
    tok = pl.BlockSpec((RET_TILE, D_MODEL), lambda b, c: (b * chunks + c, 0))
    return pl.pallas_call(
        functools.partial(_retention_kernel, chunk_decay=chunk_decay),
        grid=(bsz, chunks),
        in_specs=[tok, tok, tok, tok, _const_spec((1, D_MODEL)), _const_spec(decay.shape),
                  _const_spec(xi.shape), _const_spec(zeta.shape)],
        out_specs=tok,
        out_shape=jax.ShapeDtypeStruct(q.shape, BF16),
        scratch_shapes=[pltpu.VMEM((RET_HEADS, RET_DK, RET_DV), F32)],
        compiler_params=pltpu.CompilerParams(
            dimension_semantics=("arbitrary", "arbitrary"), vmem_limit_bytes=VMEM_LIMIT_BYTES),
        name="retention_branch",
    )(q, k, v, bz, gain, decay, xi, zeta)


def _even_out_kernel(x_ref, ya_ref, yb_ref, w_ref, o_ref):
    acc = jnp.dot(ya_ref[...], w_ref[0:S5_WIDTH, :], preferred_element_type=F32)
    acc = acc + jnp.dot(yb_ref[...], w_ref[S5_WIDTH:, :], preferred_element_type=F32)
    o_ref[...] = x_ref[...] + acc


def _even_out_proj(x2d, ya, yb, w_out):
    n = x2d.shape[0]
    tm = PROJ_TILE
    tok = pl.BlockSpec((tm, D_MODEL), lambda i: (i, 0))
    return pl.pallas_call(
        _even_out_kernel,
        grid=(n // tm,),
        in_specs=[tok, tok, tok, _const_spec(w_out.shape)],
        out_specs=tok,
        out_shape=jax.ShapeDtypeStruct(x2d.shape, F32),
        compiler_params=pltpu.CompilerParams(
            dimension_semantics=("arbitrary",), vmem_limit_bytes=VMEM_LIMIT_BYTES),
        name="even_out_proj",
    )(x2d, ya, yb, w_out)


def _odd_kernel(x_ref, g_ref, w_ref, vgain_ref, wsp_ref, bsp_ref, wout_ref, fg_ref, o_ref,
                vbuf, vnbuf, ybuf):
    x = x_ref[...]
    hn = _rms_norm(x, g_ref[...]).astype(BF16)
    gd = SGU_GROUP_DIM

    def proj(col):
        return jnp.dot(hn, w_ref[:, col:col + gd], preferred_element_type=F32)

    for g in range(SGU_GROUPS):
        vbuf[:, g * gd:(g + 1) * gd] = jax.nn.gelu(proj(SGU_WIDTH + g * gd))
    v = vbuf[...]
    mu = jnp.mean(v, axis=-1, keepdims=True)
    vc = v - mu
    var = jnp.mean(vc * vc, axis=-1, keepdims=True)
    vnbuf[...] = (vc * lax.rsqrt(var + NORM_EPS) * vgain_ref[...]).astype(BF16)

    row = lax.broadcasted_iota(jnp.int32, (SGU_CHUNK, SGU_CHUNK), 0)
    col = lax.broadcasted_iota(jnp.int32, (SGU_CHUNK, SGU_CHUNK), 1)
    for g in range(SGU_GROUPS):
        cols = slice(g * gd, (g + 1) * gd)
        ug = jax.nn.gelu(proj(g * gd))
        gate = _silu(proj(2 * SGU_WIDTH + g * gd))
        wm = jnp.where(row >= col, wsp_ref[g], 0.0).astype(BF16)
        bias = bsp_ref[g]
        for c in range(ODD_TILE // SGU_CHUNK):
            rows = slice(c * SGU_CHUNK, (c + 1) * SGU_CHUNK)
            s = jnp.dot(wm, vnbuf[rows, cols], preferred_element_type=F32) + bias
            ybuf[rows, cols] = (ug[rows] * s * gate[rows]).astype(BF16)

    xn = x + jnp.dot(ybuf[...], wout_ref[...], preferred_element_type=F32)
    o_ref[...] = _rms_norm(xn, fg_ref[...])


def _odd_layer(x2d, gain, w_in, vgain, wsp, bsp, w_out, final_gain):
    n = x2d.shape[0]
    tm = ODD_TILE
    tok = pl.BlockSpec((tm, D_MODEL), lambda i: (i, 0))
    return pl.pallas_call(
        _odd_kernel,
        grid=(n // tm,),
        in_specs=[tok, _const_spec((1, D_MODEL)), _const_spec(w_in.shape),
                  _const_spec((1, SGU_WIDTH)), _const_spec(wsp.shape), _const_spec(bsp.shape),
                  _const_spec(w_out.shape), _const_spec((1, D_MODEL))],
        out_specs=tok,
        out_shape=jax.ShapeDtypeStruct(x2d.shape, F32),
        scratch_shapes=[pltpu.VMEM((tm, SGU_WIDTH), F32),
                        pltpu.VMEM((tm, SGU_WIDTH), BF16),
                        pltpu.VMEM((tm, SGU_WIDTH), BF16)],
        compiler_params=pltpu.CompilerParams(
            dimension_semantics=("arbitrary",), vmem_limit_bytes=VMEM_LIMIT_BYTES),
        name="odd_layer",
    )(x2d, gain, w_in, vgain, wsp, bsp, w_out, final_gain)


def _rotary_tables(seq):
    half = RET_DK // 2
    pos = jnp.arange(seq, dtype=F32)
    inv = ROPE_BASE ** (-jnp.arange(half, dtype=F32) / half)
    ang = pos[:, None] * inv[None, :]
    return jnp.cos(ang), jnp.sin(ang)


def kernel(x, norm_even, w_in_even, s5_lam_re, s5_lam_im, s5_log_dt, s5_b_re, s5_b_im, s5_c_re, s5_c_im, s5_d, s5_w_glu, s5_b_glu, ret_gn_gain, w_out_even, norm_odd, w_in_odd, sgu_norm_gain, sgu_w_spatial, sgu_b_spatial, w_out_odd, final_norm):
    bsz, seq, d = x.shape
    x2d = x.reshape(bsz * seq, d)
    cos, sin = _rotary_tables(seq)

    u, az, q, k, v, bz = _even_in_proj(
        x2d, norm_even[0].reshape(1, d), w_in_even[0].astype(BF16), cos, sin, seq)
    a_tab, bmat, cmat = _s5_params(s5_lam_re[0], s5_lam_im[0], s5_log_dt[0], s5_b_re[0],
                                   s5_b_im[0], s5_c_re[0], s5_c_im[0])
    ya = _s5_branch(u, az, a_tab, bmat, cmat, s5_d[0].reshape(1, -1).astype(F32),
                    s5_w_glu[0].astype(BF16), s5_b_glu[0].reshape(1, -1).astype(F32), bsz, seq)
    yb = _retention_branch(q, k, v, bz, ret_gn_gain[0].reshape(1, -1).astype(F32), bsz, seq)
    x2d = _even_out_proj(x2d, ya, yb, w_out_even[0].astype(BF16))

    out = _odd_layer(x2d, norm_odd[0].reshape(1, d), w_in_odd[0].astype(BF16),
                     sgu_norm_gain[0].reshape(1, -1).astype(F32), sgu_w_spatial[0].astype(F32),
                     sgu_b_spatial[0].astype(F32)[:, :, None], w_out_odd[0].astype(BF16),
                     final_norm.reshape(1, d))
    return out.reshape(bsz, seq, d)
```

```python
import functools

import jax
import jax.numpy as jnp
import numpy as np
from jax import lax
from jax.experimental import pallas as pl
from jax.experimental.pallas import tpu as pltpu

F32 = jnp.float32
BF16 = jnp.bfloat16

D_MODEL = 1024
S5_WIDTH = 1024
S5_GROUP = 16
S5_GROUPS = 64
S5_STATE = 64
RET_HEADS = 4
RET_DK = 256
RET_DV = 256
ROPE_BASE = 10000.0
SGU_WIDTH = 2048
SGU_GROUPS = 4
SGU_GROUP_DIM = 512
SGU_CHUNK = 128
NORM_EPS = 1e-6

V7X_VMEM_BYTES = 64 * 1024 * 1024
VMEM_LIMIT_BYTES = V7X_VMEM_BYTES - 12 * 1024 * 1024
SUBLANES = 8
LANES = 128

PROJ_TILE = 512
S5_BLOCK = 16
S5_SEGS = SUBLANES
S5_CHUNK_GROUPS = 8
S5_CHUNK_PAIRS = S5_CHUNK_GROUPS // 2
S5_CHUNK_LANES = S5_CHUNK_GROUPS * S5_STATE
RET_TILE = 256
ODD_TILE = 512


def _const_spec(shape):
    zeros = (0,) * len(shape)
    return pl.BlockSpec(shape, lambda *_: zeros, pipeline_mode=pl.Buffered(1))


def _rms_norm(x, gain):
    ms = jnp.mean(x * x, axis=-1, keepdims=True)
    return x * lax.rsqrt(ms + NORM_EPS) * gain


def _silu(x):
    return x * jax.nn.sigmoid(x)


def _even_in_kernel(x_ref, g_ref, w_ref, cos_ref, sin_ref,
                    u_ref, az_ref, q_ref, k_ref, v_ref, bz_ref):
    hn = _rms_norm(x_ref[...], g_ref[...]).astype(BF16)

    def proj(j):
        return jnp.dot(hn, w_ref[:, j * D_MODEL:(j + 1) * D_MODEL], preferred_element_type=F32)

    def rotary_store(p, out_ref, scale):
        cos = cos_ref[...]
        sin = sin_ref[...]
        half = RET_DK // 2
        for h in range(RET_HEADS):
            x1 = p[:, h * RET_DK:h * RET_DK + half]
            x2 = p[:, h * RET_DK + half:(h + 1) * RET_DK]
            o1 = x1 * cos - x2 * sin
            o2 = x1 * sin + x2 * cos
            if scale != 1.0:
                o1 = o1 * scale
                o2 = o2 * scale
            out_ref[:, h * RET_DK:h * RET_DK + half] = o1.astype(BF16)
            out_ref[:, h * RET_DK + half:(h + 1) * RET_DK] = o2.astype(BF16)

    u_ref[...] = proj(0).astype(BF16)
    az_ref[...] = _silu(proj(1)).astype(BF16)
    rotary_store(proj(2), q_ref, 1.0)
    rotary_store(proj(3), k_ref, RET_DK ** -0.5)
    v_ref[...] = proj(4).astype(BF16)
    bz_ref[...] = _silu(proj(5)).astype(BF16)


def _even_in_proj(x2d, gain, w_in, cos, sin, seq):
    n = x2d.shape[0]
    tm = PROJ_TILE
    tiles_per_seq = seq // tm
    tok = pl.BlockSpec((tm, D_MODEL), lambda i: (i, 0))
    rot = pl.BlockSpec((tm, RET_DK // 2), lambda i: (i % tiles_per_seq, 0))
    out = jax.ShapeDtypeStruct((n, D_MODEL), BF16)
    return pl.pallas_call(
        _even_in_kernel,
        grid=(n // tm,),
        in_specs=[tok, _const_spec((1, D_MODEL)), _const_spec(w_in.shape), rot, rot],
        out_specs=[tok] * 6,
        out_shape=[out] * 6,
        compiler_params=pltpu.CompilerParams(
            dimension_semantics=("arbitrary",), vmem_limit_bytes=VMEM_LIMIT_BYTES),
        name="even_in_proj",
    )(x2d, gain, w_in, cos, sin)


def _s5_kernel(z_ref, wy_ref, bp_ref, a_ref, y_ref, vbuf, spbuf, ebuf, sinbuf, *, seg_len):
    lanes = S5_CHUNK_LANES
    re = slice(0, lanes)
    im = slice(lanes, 2 * lanes)
    for q in range(S5_CHUNK_PAIRS):
        zp = jnp.concatenate([z_ref[2 * q], z_ref[2 * q + 1]], axis=1)
        v = jnp.dot(zp, bp_ref[q], preferred_element_type=F32)
        vbuf[:, q * LANES:(q + 1) * LANES] = v[:, 0:LANES]
        vbuf[:, lanes + q * LANES:lanes + (q + 1) * LANES] = v[:, LANES:]

    ar = jnp.broadcast_to(a_ref[0:1, :], (SUBLANES, lanes))
    ai = jnp.broadcast_to(a_ref[1:2, :], (SUBLANES, lanes))

    def advance(sr, si, rows):
        return (ar * sr - ai * si + vbuf[rows, re], ar * si + ai * sr + vbuf[rows, im])

    def end_state(i, state):
        rows = pl.ds(pl.multiple_of(i * SUBLANES, SUBLANES), SUBLANES)
        return advance(*state, rows)

    zero = jnp.zeros((SUBLANES, lanes), F32)
    er, ei = lax.fori_loop(0, seg_len, end_state, (zero, zero), unroll=8)
    ebuf[:, re] = er
    ebuf[:, im] = ei

    atr = a_ref[2:3, :]
    ati = a_ref[3:4, :]
    cr = jnp.zeros((1, lanes), F32)
    ci = jnp.zeros((1, lanes), F32)
    for seg in range(S5_SEGS):
        sinbuf[seg:seg + 1, re] = cr
        sinbuf[seg:seg + 1, im] = ci
        cr, ci = (atr * cr - ati * ci + ebuf[seg:seg + 1, re],
                  atr * ci + ati * cr + ebuf[seg:seg + 1, im])

    def record(i2, state):
        sr0, si0 = state
        base = pl.multiple_of(i2 * 2 * SUBLANES, 2 * SUBLANES)
        sr1, si1 = advance(sr0, si0, pl.ds(base, SUBLANES))
        sr2, si2 = advance(sr1, si1, pl.ds(base + SUBLANES, SUBLANES))
        both = pl.ds(base, 2 * SUBLANES)
        spbuf[both, re] = jnp.concatenate([sr0, sr1], axis=0).astype(BF16)
        spbuf[both, im] = jnp.concatenate([si0, si1], axis=0).astype(BF16)
        return sr2, si2

    lax.fori_loop(0, seg_len // 2, record, (sinbuf[:, re], sinbuf[:, im]), unroll=4)

    for g in range(S5_CHUNK_GROUPS):
        q = g // 2
        lhs = jnp.concatenate([z_ref[g], spbuf[:, q * LANES:(q + 1) * LANES],
                               spbuf[:, lanes + q * LANES:lanes + (q + 1) * LANES]], axis=1)
        y_ref[g] = jnp.dot(lhs, wy_ref[g], preferred_element_type=F32).astype(BF16)


def _s5_params(lam_re, lam_im, log_dt, b_re, b_im, c_re, c_im, seg_len):
    hp = lax.Precision.HIGHEST
    r = S5_BLOCK
    g_n, p_n, h_n = S5_GROUPS, S5_STATE, S5_GROUP
    lr = jnp.minimum(lam_re.astype(F32), -1e-4)
    li = lam_im.astype(F32)
    dt = jnp.exp(log_dt.astype(F32))[:, None]
    mag = jnp.exp(lr * dt)
    ab_re = mag * jnp.cos(li * dt)
    ab_im = mag * jnp.sin(li * dt)
    den = lr * lr + li * li
    n_re = ab_re - 1.0
    n_im = ab_im
    z_re = (n_re * lr + n_im * li) / den
    z_im = (n_im * lr - n_re * li) / den
    br = b_re.astype(F32)
    bi = b_im.astype(F32)
    bb_re = z_re[..., None] * br - z_im[..., None] * bi
    bb_im = z_re[..., None] * bi + z_im[..., None] * br

    def a_pow(k):
        pm = jnp.exp(lr * dt * k)
        return pm * jnp.cos(li * dt * k), pm * jnp.sin(li * dt * k)

    kk = jnp.arange(r + 1, dtype=F32)[:, None, None]
    pr, pi = a_pow(kk)
    prk = jnp.transpose(pr, (1, 0, 2))[:, :, None, :]
    pik = jnp.transpose(pi, (1, 0, 2))[:, :, None, :]
    cr = c_re.astype(F32)[:, None]
    ci = c_im.astype(F32)[:, None]
    ca_re = cr * prk - ci * pik
    ca_im = cr * pik + ci * prk

    kt = (jnp.einsum('gkap,gph->gkah', ca_re[:, :r], bb_re, precision=hp)
          - jnp.einsum('gkap,gph->gkah', ca_im[:, :r], bb_im, precision=hp))
    jj = jnp.arange(r)
    lag = jj[None, :] - jj[:, None]
    ktg = kt[:, jnp.clip(lag, 0, r - 1)]
    ktg = jnp.where((lag >= 0)[None, :, :, None, None], ktg, 0.0)
    mz = jnp.transpose(ktg, (0, 1, 4, 2, 3)).reshape(g_n, r * h_n, r * h_n)

    cre = jnp.transpose(ca_re[:, 1:], (0, 3, 1, 2)).reshape(g_n, p_n, r * h_n)
    cim = -jnp.transpose(ca_im[:, 1:], (0, 3, 1, 2)).reshape(g_n, p_n, r * h_n)
    zc = jnp.zeros_like(cre)
    even = jnp.concatenate([cre, zc, cim, zc], axis=1)
    odd = jnp.concatenate([zc, cre, zc, cim], axis=1)
    is_odd = (jnp.arange(g_n) % 2 == 1)[:, None, None]
    wy = jnp.concatenate([mz, jnp.where(is_odd, odd, even)], axis=1).astype(BF16)

    prr = jnp.transpose(pr[r - 1::-1], (1, 0, 2))[:, :, None, :]
    pir = jnp.transpose(pi[r - 1::-1], (1, 0, 2))[:, :, None, :]
    bbr = jnp.transpose(bb_re, (0, 2, 1))[:, None]
    bbi = jnp.transpose(bb_im, (0, 2, 1))[:, None]
    bp_re = (prr * bbr - pir * bbi).reshape(g_n // 2, 2, r * h_n, p_n)
    bp_im = (prr * bbi + pir * bbr).reshape(g_n // 2, 2, r * h_n, p_n)
    zb = jnp.zeros_like(bp_re[:, 0])
    top = jnp.concatenate([bp_re[:, 0], zb, bp_im[:, 0], zb], axis=2)
    bot = jnp.concatenate([zb, bp_re[:, 1], zb, bp_im[:, 1]], axis=2)
    bp = jnp.concatenate([top, bot], axis=1).astype(BF16)

    at_re, at_im = a_pow(float(r * seg_len))
    n_chunks = g_n // S5_CHUNK_GROUPS
    a_tab = jnp.stack([x.reshape(n_chunks, S5_CHUNK_LANES) for x in (pr[r], pi[r], at_re, at_im)],
                      axis=1)
    return wy, bp, a_tab


def _s5_to_blocks(u, bsz, seq):
    seg_len = seq // S5_BLOCK // S5_SEGS
    x = u.reshape(bsz, S5_SEGS, seg_len, S5_BLOCK, S5_GROUPS, S5_GROUP)
    x = jnp.transpose(x, (4, 0, 2, 1, 3, 5))
    return x.reshape(S5_GROUPS, bsz * seq // S5_BLOCK, S5_BLOCK * S5_GROUP)


def _s5_from_blocks(y, bsz, seq):
    seg_len = seq // S5_BLOCK // S5_SEGS
    x = y.reshape(S5_GROUPS, bsz, seg_len, S5_SEGS, S5_BLOCK, S5_GROUP)
    x = jnp.transpose(x, (1, 3, 2, 4, 0, 5))
    return x.reshape(bsz * seq, S5_WIDTH)


def _s5_branch(u, wy, bp, a_tab, bsz, seq):
    n_blk = seq // S5_BLOCK
    seg_len = n_blk // S5_SEGS
    width = S5_BLOCK * S5_GROUP
    z = _s5_to_blocks(u, bsz, seq)
    blk = pl.BlockSpec((S5_CHUNK_GROUPS, n_blk, width), lambda c, b: (c, b, 0))
    y = pl.pallas_call(
        functools.partial(_s5_kernel, seg_len=seg_len),
        grid=(S5_GROUPS // S5_CHUNK_GROUPS, bsz),
        in_specs=[blk,
                  pl.BlockSpec((S5_CHUNK_GROUPS, 2 * width, width), lambda c, b: (c, 0, 0)),
                  pl.BlockSpec((S5_CHUNK_PAIRS, 2 * width, width), lambda c, b: (c, 0, 0)),
                  pl.BlockSpec((None, 4, S5_CHUNK_LANES), lambda c, b: (c, 0, 0))],
        out_specs=blk,
        out_shape=jax.ShapeDtypeStruct(z.shape, BF16),
        scratch_shapes=[
            pltpu.VMEM((n_blk, 2 * S5_CHUNK_LANES), F32),
            pltpu.VMEM((n_blk, 2 * S5_CHUNK_LANES), BF16),
            pltpu.VMEM((S5_SEGS, 2 * S5_CHUNK_LANES), F32),
            pltpu.VMEM((S5_SEGS, 2 * S5_CHUNK_LANES), F32),
        ],
        compiler_params=pltpu.CompilerParams(
            dimension_semantics=("arbitrary", "arbitrary"), vmem_limit_bytes=VMEM_LIMIT_BYTES),
        name="s5_branch",
    )(z, wy, bp, a_tab)
    return _s5_from_blocks(y, bsz, seq)


def _retention_kernel(q_ref, k_ref, v_ref, bz_ref, gain_ref, decay_ref, xi_ref, zeta_ref,
                      yb_ref, state, *, chunk_decay):
    c = pl.program_id(1)

    @pl.when(c == 0)
    def _():
        state[...] = jnp.zeros_like(state)

    for h in range(RET_HEADS):
        cols = slice(h * RET_DK, (h + 1) * RET_DK)
        qh = q_ref[:, cols]
        kh = k_ref[:, cols]
        vh = v_ref[:, cols]
        scores = lax.dot_general(qh, kh, (((1,), (1,)), ((), ())), preferred_element_type=F32)
        scores = scores * decay_ref[h]
        inner = jnp.dot(scores.astype(BF16), vh, preferred_element_type=F32)
        prev = state[h]
        qx = (qh.astype(F32) * xi_ref[:, h:h + 1]).astype(BF16)
        cross = jnp.dot(qx, prev.astype(BF16), preferred_element_type=F32)
        kz = (kh.astype(F32) * zeta_ref[:, h:h + 1]).astype(BF16)
        local = lax.dot_general(kz, vh, (((0,), (0,)), ((), ())), preferred_element_type=F32)
        state[h] = prev * chunk_decay[h] + local
        o = inner + cross
        mu = jnp.mean(o, axis=-1, keepdims=True)
        oc = o - mu
        var = jnp.mean(oc * oc, axis=-1, keepdims=True)
        o = oc * lax.rsqrt(var + NORM_EPS) * gain_ref[:, cols]
        yb_ref[:, cols] = (o * bz_ref[:, cols].astype(F32)).astype(BF16)


def _retention_tables():
    log_g = np.log1p(-np.exp2(-5.0 - np.arange(RET_HEADS, dtype=np.float64)))
    idx = np.arange(RET_TILE, dtype=np.float64)
    diff = idx[:, None] - idx[None, :]
    decay = np.where(diff >= 0, np.exp(log_g[:, None, None] * np.maximum(diff, 0.0)), 0.0)
    xi = np.exp(log_g[None, :] * (idx[:, None] + 1.0))
    zeta = np.exp(log_g[None, :] * (RET_TILE - 1.0 - idx[:, None]))
    chunk_decay = tuple(float(x) for x in np.exp(log_g * RET_TILE))
    return (jnp.asarray(decay, F32), jnp.asarray(xi, F32), jnp.asarray(zeta, F32), chunk_decay)


def _retention_branch(q, k, v, bz, gain, bsz, seq):
    chunks = seq // RET_TILE
    decay, xi, zeta, chunk_decay = _retention_tables()
    tok = pl.BlockSpec((RET_TILE, D_MODEL), lambda b, c: (b * chunks + c, 0))
    return pl.pallas_call(
        functools.partial(_retention_kernel, chunk_decay=chunk_decay),
        grid=(bsz, chunks),
        in_specs=[tok, tok, tok, tok, _const_spec((1, D_MODEL)), _const_spec(decay.shape),
                  _const_spec(xi.shape), _const_spec(zeta.shape)],
        out_specs=tok,
        out_shape=jax.ShapeDtypeStruct(q.shape, BF16),
        scratch_shapes=[pltpu.VMEM((RET_HEADS, RET_DK, RET_DV), F32)],
        compiler_params=pltpu.CompilerParams(
            dimension_semantics=("arbitrary", "arbitrary"), vmem_limit_bytes=VMEM_LIMIT_BYTES),
        name="retention_branch",
    )(q, k, v, bz, gain, decay, xi, zeta)


def _even_out_kernel(x_ref, ys_ref, u_ref, az_ref, yb_ref, d_ref, wglu_ref, bglu_ref, w_ref,
                     o_ref):
    y = ys_ref[...].astype(F32) + d_ref[...] * u_ref[...].astype(F32)
    y = jax.nn.gelu(y)
    gl = jnp.dot(y.astype(BF16), wglu_ref[...], preferred_element_type=F32) + bglu_ref[...]
    ya = (y * jax.nn.sigmoid(gl) * az_ref[...].astype(F32)).astype(BF16)
    acc = jnp.dot(ya, w_ref[0:S5_WIDTH, :], preferred_element_type=F32)
    acc = acc + jnp.dot(yb_ref[...], w_ref[S5_WIDTH:, :], preferred_element_type=F32)
    o_ref[...] = x_ref[...] + acc


def _even_out_proj(x2d, ys, u, az, yb, d_skip, w_glu, b_glu, w_out):
    n = x2d.shape[0]
    tm = PROJ_TILE
    tok = pl.BlockSpec((tm, D_MODEL), lambda i: (i, 0))
    return pl.pallas_call(
        _even_out_kernel,
        grid=(n // tm,),
        in_specs=[tok, tok, tok, tok, tok, _const_spec((1, S5_WIDTH)), _const_spec(w_glu.shape),
                  _const_spec((1, S5_WIDTH)), _const_spec(w_out.shape)],
        out_specs=tok,
        out_shape=jax.ShapeDtypeStruct(x2d.shape, F32),
        compiler_params=pltpu.CompilerParams(
            dimension_semantics=("arbitrary",), vmem_limit_bytes=VMEM_LIMIT_BYTES),
        name="even_out_proj",
    )(x2d, ys, u, az, yb, d_skip, w_glu, b_glu, w_out)


def _odd_kernel(x_ref, g_ref, w_ref, vgain_ref, wsp_ref, bsp_ref, wout_ref, fg_ref, o_ref,
                vbuf, vnbuf, ybuf):
    x = x_ref[...]
    hn = _rms_norm(x, g_ref[...]).astype(BF16)
    gd = SGU_GROUP_DIM

    def proj(col):
        return jnp.dot(hn, w_ref[:, col:col + gd], preferred_element_type=F32)

    for g in range(SGU_GROUPS):
        vbuf[:, g * gd:(g + 1) * gd] = jax.nn.gelu(proj(SGU_WIDTH + g * gd))
    v = vbuf[...]
    mu = jnp.mean(v, axis=-1, keepdims=True)
    vc = v - mu
    var = jnp.mean(vc * vc, axis=-1, keepdims=True)
    vnbuf[...] = (vc * lax.rsqrt(var + NORM_EPS) * vgain_ref[...]).astype(BF16)

    row = lax.broadcasted_iota(jnp.int32, (SGU_CHUNK, SGU_CHUNK), 0)
    col = lax.broadcasted_iota(jnp.int32, (SGU_CHUNK, SGU_CHUNK), 1)
    for g in range(SGU_GROUPS):
        cols = slice(g * gd, (g + 1) * gd)
        ug = jax.nn.gelu(proj(g * gd))
        gate = _silu(proj(2 * SGU_WIDTH + g * gd))
        wm = jnp.where(row >= col, wsp_ref[g], 0.0).astype(BF16)
        bias = bsp_ref[g]
        for c in range(ODD_TILE // SGU_CHUNK):
            rows = slice(c * SGU_CHUNK, (c + 1) * SGU_CHUNK)
            s = jnp.dot(wm, vnbuf[rows, cols], preferred_element_type=F32) + bias
            ybuf[rows, cols] = (ug[rows] * s * gate[rows]).astype(BF16)

    xn = x + jnp.dot(ybuf[...], wout_ref[...], preferred_element_type=F32)
    o_ref[...] = _rms_norm(xn, fg_ref[...])


def _odd_layer(x2d, gain, w_in, vgain, wsp, bsp, w_out, final_gain):
    n = x2d.shape[0]
    tm = ODD_TILE
    tok = pl.BlockSpec((tm, D_MODEL), lambda i: (i, 0))
    return pl.pallas_call(
        _odd_kernel,
        grid=(n // tm,),
        in_specs=[tok, _const_spec((1, D_MODEL)), _const_spec(w_in.shape),
                  _const_spec((1, SGU_WIDTH)), _const_spec(wsp.shape), _const_spec(bsp.shape),
                  _const_spec(w_out.shape), _const_spec((1, D_MODEL))],
        out_specs=tok,
        out_shape=jax.ShapeDtypeStruct(x2d.shape, F32),
        scratch_shapes=[pltpu.VMEM((tm, SGU_WIDTH), F32),
                        pltpu.VMEM((tm, SGU_WIDTH), BF16),
                        pltpu.VMEM((tm, SGU_WIDTH), BF16)],
        compiler_params=pltpu.CompilerParams(
            dimension_semantics=("arbitrary",), vmem_limit_bytes=VMEM_LIMIT_BYTES),
        name="odd_layer",
    )(x2d, gain, w_in, vgain, wsp, bsp, w_out, final_gain)


def _rotary_tables(seq):
    half = RET_DK // 2
    pos = jnp.arange(seq, dtype=F32)
    inv = ROPE_BASE ** (-jnp.arange(half, dtype=F32) / half)
    ang = pos[:, None] * inv[None, :]
    return jnp.cos(ang), jnp.sin(ang)


def kernel(x, norm_even, w_in_even, s5_lam_re, s5_lam_im, s5_log_dt, s5_b_re, s5_b_im, s5_c_re, s5_c_im, s5_d, s5_w_glu, s5_b_glu, ret_gn_gain, w_out_even, norm_odd, w_in_odd, sgu_norm_gain, sgu_w_spatial, sgu_b_spatial, w_out_odd, final_norm):
    bsz, seq, d = x.shape
    x2d = x.reshape(bsz * seq, d)
    cos, sin = _rotary_tables(seq)

    u, az, q, k, v, bz = _even_in_proj(
        x2d, norm_even[0].reshape(1, d), w_in_even[0].astype(BF16), cos, sin, seq)
    seg_len = seq // S5_BLOCK // S5_SEGS
    wy, bp, a_tab = _s5_params(s5_lam_re[0], s5_lam_im[0], s5_log_dt[0], s5_b_re[0], s5_b_im[0],
                               s5_c_re[0], s5_c_im[0], seg_len)
    ys = _s5_branch(u, wy, bp, a_tab, bsz, seq)
    yb = _retention_branch(q, k, v, bz, ret_gn_gain[0].reshape(1, -1).astype(F32), bsz, seq)
    x2d = _even_out_proj(x2d, ys, u, az, yb, s5_d[0].reshape(1, -1).astype(F32),
                         s5_w_glu[0].astype(BF16), s5_b_glu[0].reshape(1, -1).astype(F32),
                         w_out_even[0].astype(BF16))

    out = _odd_layer(x2d, norm_odd[0].reshape(1, d), w_in_odd[0].astype(BF16),
                     sgu_norm_gain[0].reshape(1, -1).astype(F32), sgu_w_spatial[0].astype(F32),
                     sgu_b_spatial[0].astype(F32)[:, :, None], w_out_odd[0].astype(BF16),
                     final_norm.reshape(1, d))
    return out.reshape(bsz, seq, d)
```

```python
import functools

import jax
import jax.numpy as jnp
import numpy as np
from jax import lax
from jax.experimental import pallas as pl
from jax.experimental.pallas import tpu as pltpu

F32 = jnp.float32
BF16 = jnp.bfloat16

D_MODEL = 1024
S5_WIDTH = 1024
S5_GROUP = 16
S5_GROUPS = 64
S5_STATE = 64
RET_HEADS = 4
RET_DK = 256
RET_DV = 256
ROPE_BASE = 10000.0
SGU_WIDTH = 2048
SGU_GROUPS = 4
SGU_GROUP_DIM = 512
SGU_CHUNK = 128
NORM_EPS = 1e-6

V7X_VMEM_BYTES = 64 * 1024 * 1024
VMEM_LIMIT_BYTES = V7X_VMEM_BYTES - 12 * 1024 * 1024
SUBLANES = 8
LANES = 128

PROJ_TILE = 512
S5_BLOCK = 16
S5_SEGS = SUBLANES
S5_CHUNK_GROUPS = 8
S5_CHUNK_PAIRS = S5_CHUNK_GROUPS // 2
S5_CHUNK_LANES = S5_CHUNK_GROUPS * S5_STATE
S5_SEG_PITCH = 72
RET_TILE = 256
ODD_TILE = 512


def _const_spec(shape):
    zeros = (0,) * len(shape)
    return pl.BlockSpec(shape, lambda *_: zeros, pipeline_mode=pl.Buffered(1))


def _rms_norm(x, gain):
    ms = jnp.mean(x * x, axis=-1, keepdims=True)
    return x * lax.rsqrt(ms + NORM_EPS) * gain


def _silu(x):
    return x * jax.nn.sigmoid(x)


def _even_in_kernel(x_ref, g_ref, w_ref, cos_ref, sin_ref,
                    u_ref, az_ref, q_ref, k_ref, v_ref, bz_ref):
    hn = _rms_norm(x_ref[...], g_ref[...]).astype(BF16)

    def proj(j):
        return jnp.dot(hn, w_ref[:, j * D_MODEL:(j + 1) * D_MODEL], preferred_element_type=F32)

    def rotary_store(p, out_ref, scale):
        cos = cos_ref[...]
        sin = sin_ref[...]
        half = RET_DK // 2
        for h in range(RET_HEADS):
            x1 = p[:, h * RET_DK:h * RET_DK + half]
            x2 = p[:, h * RET_DK + half:(h + 1) * RET_DK]
            o1 = x1 * cos - x2 * sin
            o2 = x1 * sin + x2 * cos
            if scale != 1.0:
                o1 = o1 * scale
                o2 = o2 * scale
            out_ref[:, h * RET_DK:h * RET_DK + half] = o1.astype(BF16)
            out_ref[:, h * RET_DK + half:(h + 1) * RET_DK] = o2.astype(BF16)

    u_ref[...] = proj(0).astype(BF16)
    az_ref[...] = _silu(proj(1)).astype(BF16)
    rotary_store(proj(2), q_ref, 1.0)
    rotary_store(proj(3), k_ref, RET_DK ** -0.5)
    v_ref[...] = proj(4).astype(BF16)
    bz_ref[...] = _silu(proj(5)).astype(BF16)


def _even_in_proj(x2d, gain, w_in, cos, sin, seq):
    n = x2d.shape[0]
    tm = PROJ_TILE
    tiles_per_seq = seq // tm
    tok = pl.BlockSpec((tm, D_MODEL), lambda i: (i, 0))
    rot = pl.BlockSpec((tm, RET_DK // 2), lambda i: (i % tiles_per_seq, 0))
    out = jax.ShapeDtypeStruct((n, D_MODEL), BF16)
    return pl.pallas_call(
        _even_in_kernel,
        grid=(n // tm,),
        in_specs=[tok, _const_spec((1, D_MODEL)), _const_spec(w_in.shape), rot, rot],
        out_specs=[tok] * 6,
        out_shape=[out] * 6,
        compiler_params=pltpu.CompilerParams(
            dimension_semantics=("arbitrary",), vmem_limit_bytes=VMEM_LIMIT_BYTES),
        name="even_in_proj",
    )(x2d, gain, w_in, cos, sin)


def _s5_kernel(u_ref, wyt_ref, bp_ref, a_ref, y_ref, tokbuf, zt, ytbuf, vbuf, spbuf, ebuf, sinbuf,
               *, n_blk, seg_len):
    lanes = S5_CHUNK_LANES
    re = slice(0, lanes)
    im = slice(lanes, 2 * lanes)
    pairs = S5_CHUNK_PAIRS

    tokbuf[...] = u_ref[...].astype(F32)
    for j in range(S5_BLOCK):
        ujt = tokbuf[pl.ds(j, n_blk, stride=S5_BLOCK), :].T.astype(BF16)
        for gl in range(S5_CHUNK_GROUPS):
            zt[gl, j * S5_GROUP:(j + 1) * S5_GROUP, :] = ujt[gl * S5_GROUP:(gl + 1) * S5_GROUP, :]

    for q in range(pairs):
        zc = jnp.concatenate([zt[2 * q], zt[2 * q + 1]], axis=0)
        v = lax.dot_general(zc, bp_ref[q], (((0,), (0,)), ((), ())),
                            preferred_element_type=F32)
        for seg in range(S5_SEGS):
            src = slice(seg * seg_len, (seg + 1) * seg_len)
            dst = slice(seg * S5_SEG_PITCH, seg * S5_SEG_PITCH + seg_len)
            vbuf[q, dst, :] = v[src, 0:LANES]
            vbuf[pairs + q, dst, :] = v[src, LANES:]

    ar = jnp.broadcast_to(a_ref[0:1, :], (SUBLANES, lanes))
    ai = jnp.broadcast_to(a_ref[1:2, :], (SUBLANES, lanes))

    def advance(sr, si, i):
        rows = pl.ds(i, S5_SEGS, stride=S5_SEG_PITCH)
        xr = jnp.concatenate([vbuf[s, rows, :] for s in range(pairs)], axis=1)
        xi = jnp.concatenate([vbuf[pairs + s, rows, :] for s in range(pairs)], axis=1)
        return ar * sr - ai * si + xr, ar * si + ai * sr + xi

    def end_state(i, state):
        return advance(*state, i)

    zero = jnp.zeros((SUBLANES, lanes), F32)
    er, ei = lax.fori_loop(0, seg_len, end_state, (zero, zero), unroll=8)
    ebuf[:, re] = er
    ebuf[:, im] = ei

    atr = a_ref[2:3, :]
    ati = a_ref[3:4, :]
    cr = jnp.zeros((1, lanes), F32)
    ci = jnp.zeros((1, lanes), F32)
    for seg in range(S5_SEGS):
        sinbuf[seg:seg + 1, re] = cr
        sinbuf[seg:seg + 1, im] = ci
        cr, ci = (atr * cr - ati * ci + ebuf[seg:seg + 1, re],
                  atr * ci + ati * cr + ebuf[seg:seg + 1, im])

    def record(i, state):
        sr, si = state
        rows = pl.ds(i, S5_SEGS, stride=S5_SEG_PITCH)
        for s in range(pairs):
            spbuf[s, rows, :] = sr[:, s * LANES:(s + 1) * LANES]
            spbuf[pairs + s, rows, :] = si[:, s * LANES:(s + 1) * LANES]
        return advance(sr, si, i)

    lax.fori_loop(0, seg_len, record, (sinbuf[:, re], sinbuf[:, im]), unroll=8)

    def natural_rows(slab):
        return jnp.concatenate(
            [spbuf[slab, seg * S5_SEG_PITCH:seg * S5_SEG_PITCH + seg_len, :]
             for seg in range(S5_SEGS)], axis=0)

    for q in range(pairs):
        sp = jnp.concatenate([natural_rows(q), natural_rows(pairs + q)], axis=1)
        spt = sp.astype(BF16).T
        for gl in (2 * q, 2 * q + 1):
            rhs = jnp.concatenate([zt[gl], spt], axis=0)
            ytbuf[gl] = jnp.dot(wyt_ref[gl], rhs, preferred_element_type=F32)

    for j in range(S5_BLOCK):
        yt = jnp.concatenate([ytbuf[gl, j * S5_GROUP:(j + 1) * S5_GROUP, :]
                              for gl in range(S5_CHUNK_GROUPS)], axis=0)
        tokbuf[pl.ds(j, n_blk, stride=S5_BLOCK), :] = yt.T
    y_ref[...] = tokbuf[...].astype(BF16)


def _s5_params(lam_re, lam_im, log_dt, b_re, b_im, c_re, c_im, seg_len):
    hp = lax.Precision.HIGHEST
    r = S5_BLOCK
    g_n, p_n, h_n = S5_GROUPS, S5_STATE, S5_GROUP
    lr = jnp.minimum(lam_re.astype(F32), -1e-4)
    li = lam_im.astype(F32)
    dt = jnp.exp(log_dt.astype(F32))[:, None]
    mag = jnp.exp(lr * dt)
    ab_re = mag * jnp.cos(li * dt)
    ab_im = mag * jnp.sin(li * dt)
    den = lr * lr + li * li
    n_re = ab_re - 1.0
    n_im = ab_im
    z_re = (n_re * lr + n_im * li) / den
    z_im = (n_im * lr - n_re * li) / den
    br = b_re.astype(F32)
    bi = b_im.astype(F32)
    bb_re = z_re[..., None] * br - z_im[..., None] * bi
    bb_im = z_re[..., None] * bi + z_im[..., None] * br

    def a_pow(k):
        pm = jnp.exp(lr * dt * k)
        return pm * jnp.cos(li * dt * k), pm * jnp.sin(li * dt * k)

    kk = jnp.arange(r + 1, dtype=F32)[:, None, None]
    pr, pi = a_pow(kk)
    prk = jnp.transpose(pr, (1, 0, 2))[:, :, None, :]
    pik = jnp.transpose(pi, (1, 0, 2))[:, :, None, :]
    cr = c_re.astype(F32)[:, None]
    ci = c_im.astype(F32)[:, None]
    ca_re = cr * prk - ci * pik
    ca_im = cr * pik + ci * prk

    kt = (jnp.einsum('gkap,gph->gkah', ca_re[:, :r], bb_re, precision=hp)
          - jnp.einsum('gkap,gph->gkah', ca_im[:, :r], bb_im, precision=hp))
    jj = np.arange(r)
    shift = (jj[None, None, :] - jj[None, :, None] == jj[:, None, None]).astype(np.float32)
    mzt = jnp.einsum('ljk,glah->gkajh', shift, kt).reshape(g_n, r * h_n, r * h_n)

    cre = ca_re[:, 1:].reshape(g_n, r * h_n, p_n)
    cim = -ca_im[:, 1:].reshape(g_n, r * h_n, p_n)
    zc = jnp.zeros_like(cre)
    even = jnp.concatenate([cre, zc, cim, zc], axis=2)
    odd = jnp.concatenate([zc, cre, zc, cim], axis=2)
    is_odd = (jnp.arange(g_n) % 2 == 1)[:, None, None]
    wyt = jnp.concatenate([mzt, jnp.where(is_odd, odd, even)], axis=2).astype(BF16)

    prr = jnp.transpose(pr[r - 1::-1], (1, 0, 2))[:, :, None, :]
    pir = jnp.transpose(pi[r - 1::-1], (1, 0, 2))[:, :, None, :]
    bbr = jnp.transpose(bb_re, (0, 2, 1))[:, None]
    bbi = jnp.transpose(bb_im, (0, 2, 1))[:, None]
    bp_re = (prr * bbr - pir * bbi).reshape(g_n // 2, 2, r * h_n, p_n)
    bp_im = (prr * bbi + pir * bbr).reshape(g_n // 2, 2, r * h_n, p_n)
    zb = jnp.zeros_like(bp_re[:, 0])
    top = jnp.concatenate([bp_re[:, 0], zb, bp_im[:, 0], zb], axis=2)
    bot = jnp.concatenate([zb, bp_re[:, 1], zb, bp_im[:, 1]], axis=2)
    bp = jnp.concatenate([top, bot], axis=1).astype(BF16)

    at_re, at_im = a_pow(float(r * seg_len))
    n_chunks = g_n // S5_CHUNK_GROUPS
    a_tab = jnp.stack([x.reshape(n_chunks, S5_CHUNK_LANES) for x in (pr[r], pi[r], at_re, at_im)],
                      axis=1)
    return wyt, bp, a_tab


def _s5_branch(u, wyt, bp, a_tab, bsz, seq):
    n_blk = seq // S5_BLOCK
    seg_len = n_blk // S5_SEGS
    width = S5_BLOCK * S5_GROUP
    slab = S5_CHUNK_GROUPS * S5_GROUP
    scan_rows = S5_SEGS * S5_SEG_PITCH
    tok = pl.BlockSpec((seq, slab), lambda c, b: (b, c))
    return pl.pallas_call(
        functools.partial(_s5_kernel, n_blk=n_blk, seg_len=seg_len),
        grid=(S5_GROUPS // S5_CHUNK_GROUPS, bsz),
        in_specs=[tok,
                  pl.BlockSpec((S5_CHUNK_GROUPS, width, 2 * width), lambda c, b: (c, 0, 0)),
                  pl.BlockSpec((S5_CHUNK_PAIRS, 2 * width, width), lambda c, b: (c, 0, 0)),
                  pl.BlockSpec((None, 4, S5_CHUNK_LANES), lambda c, b: (c, 0, 0))],
        out_specs=tok,
        out_shape=jax.ShapeDtypeStruct(u.shape, BF16),
        scratch_shapes=[
            pltpu.VMEM((seq, slab), F32),
            pltpu.VMEM((S5_CHUNK_GROUPS, width, n_blk), BF16),
            pltpu.VMEM((S5_CHUNK_GROUPS, width, n_blk), F32),
            pltpu.VMEM((2 * S5_CHUNK_PAIRS, scan_rows, LANES), F32),
            pltpu.VMEM((2 * S5_CHUNK_PAIRS, scan_rows, LANES), F32),
            pltpu.VMEM((S5_SEGS, 2 * S5_CHUNK_LANES), F32),
            pltpu.VMEM((S5_SEGS, 2 * S5_CHUNK_LANES), F32),
        ],
        compiler_params=pltpu.CompilerParams(
            dimension_semantics=("arbitrary", "arbitrary"), vmem_limit_bytes=VMEM_LIMIT_BYTES),
        name="s5_branch",
    )(u, wyt, bp, a_tab)


def _retention_kernel(q_ref, k_ref, v_ref, bz_ref, gain_ref, decay_ref, xi_ref, zeta_ref,
                      yb_ref, state, *, chunk_decay):
    c = pl.program_id(1)

    @pl.when(c == 0)
    def _():
        state[...] = jnp.zeros_like(state)

    for h in range(RET_HEADS):
        cols = slice(h * RET_DK, (h + 1) * RET_DK)
        qh = q_ref[:, cols]
        kh = k_ref[:, cols]
        vh = v_ref[:, cols]
        scores = lax.dot_general(qh, kh, (((1,), (1,)), ((), ())), preferred_element_type=F32)
        scores = scores * decay_ref[h]
        inner = jnp.dot(scores.astype(BF16), vh, preferred_element_type=F32)
        prev = state[h]
        qx = (qh.astype(F32) * xi_ref[:, h:h + 1]).astype(BF16)
        cross = jnp.dot(qx, prev.astype(BF16), preferred_element_type=F32)
        kz = (kh.astype(F32) * zeta_ref[:, h:h + 1]).astype(BF16)
        local = lax.dot_general(kz, vh, (((0,), (0,)), ((), ())), preferred_element_type=F32)
        state[h] = prev * chunk_decay[h] + local
        o = inner + cross
        mu = jnp.mean(o, axis=-1, keepdims=True)
        oc = o - mu
        var = jnp.mean(oc * oc, axis=-1, keepdims=True)
        o = oc * lax.rsqrt(var + NORM_EPS) * gain_ref[:, cols]
        yb_ref[:, cols] = (o * bz_ref[:, cols].astype(F32)).astype(BF16)


def _retention_tables():
    log_g = np.log1p(-np.exp2(-5.0 - np.arange(RET_HEADS, dtype=np.float64)))
    idx = np.arange(RET_TILE, dtype=np.float64)
    diff = idx[:, None] - idx[None, :]
    decay = np.where(diff >= 0, np.exp(log_g[:, None, None] * np.maximum(diff, 0.0)), 0.0)
    xi = np.exp(log_g[None, :] * (idx[:, None] + 1.0))
    zeta = np.exp(log_g[None, :] * (RET_TILE - 1.0 - idx[:, None]))
    chunk_decay = tuple(float(x) for x in np.exp(log_g * RET_TILE))
    return (jnp.asarray(decay, F32), jnp.asarray(xi, F32), jnp.asarray(zeta, F32), chunk_decay)


def _retention_branch(q, k, v, bz, gain, bsz, seq):
    chunks = seq // RET_TILE
    decay, xi, zeta, chunk_decay = _retention_tables()
    tok = pl.BlockSpec((RET_TILE, D_MODEL), lambda b, c: (b * chunks + c, 0))
    return pl.pallas_call(
        functools.partial(_retention_kernel, chunk_decay=chunk_decay),
        grid=(bsz, chunks),
        in_specs=[tok, tok, tok, tok, _const_spec((1, D_MODEL)), _const_spec(decay.shape),
                  _const_spec(xi.shape), _const_spec(zeta.shape)],
        out_specs=tok,
        out_shape=jax.ShapeDtypeStruct(q.shape, BF16),
        scratch_shapes=[pltpu.VMEM((RET_HEADS, RET_DK, RET_DV), F32)],
        compiler_params=pltpu.CompilerParams(
            dimension_semantics=("arbitrary", "arbitrary"), vmem_limit_bytes=VMEM_LIMIT_BYTES),
        name="retention_branch",
    )(q, k, v, bz, gain, decay, xi, zeta)


def _even_out_kernel(x_ref, ys_ref, u_ref, az_ref, yb_ref, d_ref, wglu_ref, bglu_ref, w_ref,
                     o_ref):
    y = ys_ref[...].astype(F32) + d_ref[...] * u_ref[...].astype(F32)
    y = jax.nn.gelu(y)
    gl = jnp.dot(y.astype(BF16), wglu_ref[...], preferred_element_type=F32) + bglu_ref[...]
    ya = (y * jax.nn.sigmoid(gl) * az_ref[...].astype(F32)).astype(BF16)
    acc = jnp.dot(ya, w_ref[0:S5_WIDTH, :], preferred_element_type=F32)
    acc = acc + jnp.dot(yb_ref[...], w_ref[S5_WIDTH:, :], preferred_element_type=F32)
    o_ref[...] = x_ref[...] + acc


def _even_out_proj(x2d, ys, u, az, yb, d_skip, w_glu, b_glu, w_out):
    n = x2d.shape[0]
    tm = PROJ_TILE
    tok = pl.BlockSpec((tm, D_MODEL), lambda i: (i, 0))
    return pl.pallas_call(
        _even_out_kernel,
        grid=(n // tm,),
        in_specs=[tok, tok, tok, tok, tok, _const_spec((1, S5_WIDTH)), _const_spec(w_glu.shape),
                  _const_spec((1, S5_WIDTH)), _const_spec(w_out.shape)],
        out_specs=tok,
        out_shape=jax.ShapeDtypeStruct(x2d.shape, F32),
        compiler_params=pltpu.CompilerParams(
            dimension_semantics=("arbitrary",), vmem_limit_bytes=VMEM_LIMIT_BYTES),
        name="even_out_proj",
    )(x2d, ys, u, az, yb, d_skip, w_glu, b_glu, w_out)


def _odd_kernel(x_ref, g_ref, w_ref, vgain_ref, wsp_ref, bsp_ref, wout_ref, fg_ref, o_ref,
                vbuf, vnbuf, ybuf):
    x = x_ref[...]
    hn = _rms_norm(x, g_ref[...]).astype(BF16)
    gd = SGU_GROUP_DIM

    def proj(col):
        return jnp.dot(hn, w_ref[:, col:col + gd], preferred_element_type=F32)

    for g in range(SGU_GROUPS):
        vbuf[:, g * gd:(g + 1) * gd] = jax.nn.gelu(proj(SGU_WIDTH + g * gd))
    v = vbuf[...]
    mu = jnp.mean(v, axis=-1, keepdims=True)
    vc = v - mu
    var = jnp.mean(vc * vc, axis=-1, keepdims=True)
    vnbuf[...] = (vc * lax.rsqrt(var + NORM_EPS) * vgain_ref[...]).astype(BF16)

    row = lax.broadcasted_iota(jnp.int32, (SGU_CHUNK, SGU_CHUNK), 0)
    col = lax.broadcasted_iota(jnp.int32, (SGU_CHUNK, SGU_CHUNK), 1)
    for g in range(SGU_GROUPS):
        cols = slice(g * gd, (g + 1) * gd)
        ug = jax.nn.gelu(proj(g * gd))
        gate = _silu(proj(2 * SGU_WIDTH + g * gd))
        wm = jnp.where(row >= col, wsp_ref[g], 0.0).astype(BF16)
        bias = bsp_ref[g]
        for c in range(ODD_TILE // SGU_CHUNK):
            rows = slice(c * SGU_CHUNK, (c + 1) * SGU_CHUNK)
            s = jnp.dot(wm, vnbuf[rows, cols], preferred_element_type=F32) + bias
            ybuf[rows, cols] = (ug[rows] * s * gate[rows]).astype(BF16)

    xn = x + jnp.dot(ybuf[...], wout_ref[...], preferred_element_type=F32)
    o_ref[...] = _rms_norm(xn, fg_ref[...])


def _odd_layer(x2d, gain, w_in, vgain, wsp, bsp, w_out, final_gain):
    n = x2d.shape[0]
    tm = ODD_TILE
    tok = pl.BlockSpec((tm, D_MODEL), lambda i: (i, 0))
    return pl.pallas_call(
        _odd_kernel,
        grid=(n // tm,),
        in_specs=[tok, _const_spec((1, D_MODEL)), _const_spec(w_in.shape),
                  _const_spec((1, SGU_WIDTH)), _const_spec(wsp.shape), _const_spec(bsp.shape),
                  _const_spec(w_out.shape), _const_spec((1, D_MODEL))],
        out_specs=tok,
        out_shape=jax.ShapeDtypeStruct(x2d.shape, F32),
        scratch_shapes=[pltpu.VMEM((tm, SGU_WIDTH), F32),
                        pltpu.VMEM((tm, SGU_WIDTH), BF16),
                        pltpu.VMEM((tm, SGU_WIDTH), BF16)],
        compiler_params=pltpu.CompilerParams(
            dimension_semantics=("arbitrary",), vmem_limit_bytes=VMEM_LIMIT_BYTES),
        name="odd_layer",
    )(x2d, gain, w_in, vgain, wsp, bsp, w_out, final_gain)


def _rotary_tables(seq):
    half = RET_DK // 2
    pos = jnp.arange(seq, dtype=F32)
    inv = ROPE_BASE ** (-jnp.arange(half, dtype=F32) / half)
    ang = pos[:, None] * inv[None, :]
    return jnp.cos(ang), jnp.sin(ang)


def kernel(x, norm_even, w_in_even, s5_lam_re, s5_lam_im, s5_log_dt, s5_b_re, s5_b_im, s5_c_re, s5_c_im, s5_d, s5_w_glu, s5_b_glu, ret_gn_gain, w_out_even, norm_odd, w_in_odd, sgu_norm_gain, sgu_w_spatial, sgu_b_spatial, w_out_odd, final_norm):
    bsz, seq, d = x.shape
    x2d = x.reshape(bsz * seq, d)
    cos, sin = _rotary_tables(seq)

    u, az, q, k, v, bz = _even_in_proj(
        x2d, norm_even[0].reshape(1, d), w_in_even[0].astype(BF16), cos, sin, seq)
    seg_len = seq // S5_BLOCK // S5_SEGS
    wyt, bp, a_tab = _s5_params(s5_lam_re[0], s5_lam_im[0], s5_log_dt[0], s5_b_re[0], s5_b_im[0],
                                s5_c_re[0], s5_c_im[0], seg_len)
    ys = _s5_branch(u, wyt, bp, a_tab, bsz, seq)
    yb = _retention_branch(q, k, v, bz, ret_gn_gain[0].reshape(1, -1).astype(F32), bsz, seq)
    x2d = _even_out_proj(x2d, ys, u, az, yb, s5_d[0].reshape(1, -1).astype(F32),
                         s5_w_glu[0].astype(BF16), s5_b_glu[0].reshape(1, -1).astype(F32),
                         w_out_even[0].astype(BF16))

    out = _odd_layer(x2d, norm_odd[0].reshape(1, d), w_in_odd[0].astype(BF16),
                     sgu_norm_gain[0].reshape(1, -1).astype(F32), sgu_w_spatial[0].astype(F32),
                     sgu_b_spatial[0].astype(F32)[:, :, None], w_out_odd[0].astype(BF16),
                     final_norm.reshape(1, d))
    return out.reshape(bsz, seq, d)
```

```python
import functools

import jax
import jax.numpy as jnp
import numpy as np
from jax import lax
from jax.experimental import pallas as pl
from jax.experimental.pallas import tpu as pltpu

F32 = jnp.float32
BF16 = jnp.bfloat16

D_MODEL = 1024
S5_WIDTH = 1024
S5_GROUP = 16
S5_GROUPS = 64
S5_STATE = 64
RET_HEADS = 4
RET_DK = 256
RET_DV = 256
ROPE_BASE = 10000.0
SGU_WIDTH = 2048
SGU_GROUPS = 4
SGU_GROUP_DIM = 512
SGU_CHUNK = 128
NORM_EPS = 1e-6

V7X_VMEM_BYTES = 64 * 1024 * 1024
VMEM_LIMIT_BYTES = V7X_VMEM_BYTES - 12 * 1024 * 1024
SUBLANES = 8
LANES = 128

PROJ_TILE = 512
S5_BLOCK = 16
S5_SEGS = SUBLANES
S5_CHUNK_GROUPS = 8
S5_CHUNK_PAIRS = S5_CHUNK_GROUPS // 2
S5_CHUNK_LANES = S5_CHUNK_GROUPS * S5_STATE
S5_SLABS = S5_WIDTH // LANES
S5_SEG_PITCH = 72
RET_TILE = 256
ODD_TILE = 512


def _const_spec(shape):
    zeros = (0,) * len(shape)
    return pl.BlockSpec(shape, lambda *_: zeros, pipeline_mode=pl.Buffered(1))


def _rms_norm(x, gain):
    ms = jnp.mean(x * x, axis=-1, keepdims=True)
    return x * lax.rsqrt(ms + NORM_EPS) * gain


def _silu(x):
    return x * jax.nn.sigmoid(x)


def _even_in_kernel(x_ref, g_ref, w_ref, cos_ref, sin_ref, zeta_ref,
                    u_ref, az_ref, q_ref, k_ref, kz_ref, v_ref, bz_ref):
    hn = _rms_norm(x_ref[...], g_ref[...]).astype(BF16)

    def proj(j):
        return jnp.dot(hn, w_ref[:, j * D_MODEL:(j + 1) * D_MODEL], preferred_element_type=F32)

    def rotary_store(p, out_ref, scale, decayed_ref=None):
        cos = cos_ref[...]
        sin = sin_ref[...]
        half = RET_DK // 2
        for h in range(RET_HEADS):
            lo = slice(h * RET_DK, h * RET_DK + half)
            hi = slice(h * RET_DK + half, (h + 1) * RET_DK)
            x1 = p[:, lo]
            x2 = p[:, hi]
            o1 = x1 * cos - x2 * sin
            o2 = x1 * sin + x2 * cos
            if scale != 1.0:
                o1 = o1 * scale
                o2 = o2 * scale
            out_ref[:, lo] = o1.astype(BF16)
            out_ref[:, hi] = o2.astype(BF16)
            if decayed_ref is not None:
                decayed_ref[:, lo] = (o1 * zeta_ref[:, lo]).astype(BF16)
                decayed_ref[:, hi] = (o2 * zeta_ref[:, hi]).astype(BF16)

    u = proj(0).astype(BF16)
    for s in range(S5_SLABS):
        u_ref[s] = u[:, s * LANES:(s + 1) * LANES]
    az_ref[...] = _silu(proj(1)).astype(BF16)
    rotary_store(proj(2), q_ref, 1.0)
    rotary_store(proj(3), k_ref, RET_DK ** -0.5, kz_ref)
    v_ref[...] = proj(4).astype(BF16)
    bz_ref[...] = _silu(proj(5)).astype(BF16)


def _even_in_proj(x2d, gain, w_in, cos, sin, zeta, seq):
    n = x2d.shape[0]
    tm = PROJ_TILE
    tiles_per_seq = seq // tm
    tok = pl.BlockSpec((tm, D_MODEL), lambda i: (i, 0))
    rot = pl.BlockSpec((tm, RET_DK // 2), lambda i: (i % tiles_per_seq, 0))
    out = jax.ShapeDtypeStruct((n, D_MODEL), BF16)
    return pl.pallas_call(
        _even_in_kernel,
        grid=(n // tm,),
        in_specs=[tok, _const_spec((1, D_MODEL)), _const_spec(w_in.shape), rot, rot,
                  _const_spec(zeta.shape)],
        out_specs=[pl.BlockSpec((S5_SLABS, tm, LANES), lambda i: (0, i, 0))] + [tok] * 6,
        out_shape=[jax.ShapeDtypeStruct((S5_SLABS, n, LANES), BF16)] + [out] * 6,
        compiler_params=pltpu.CompilerParams(
            dimension_semantics=("arbitrary",), vmem_limit_bytes=VMEM_LIMIT_BYTES),
        name="even_in_proj",
    )(x2d, gain, w_in, cos, sin, zeta)


def _s5_kernel(u_ref, wyt_ref, bp_ref, a_ref, y_ref, tokbuf, zt, ytbuf, vbuf, spbuf, ebuf, sinbuf,
               *, n_blk, seg_len):
    lanes = S5_CHUNK_LANES
    re = slice(0, lanes)
    im = slice(lanes, 2 * lanes)
    pairs = S5_CHUNK_PAIRS

    tokbuf[...] = u_ref[...].astype(F32)
    for j in range(S5_BLOCK):
        ujt = tokbuf[pl.ds(j, n_blk, stride=S5_BLOCK), :].T.astype(BF16)
        for gl in range(S5_CHUNK_GROUPS):
            zt[gl, j * S5_GROUP:(j + 1) * S5_GROUP, :] = ujt[gl * S5_GROUP:(gl + 1) * S5_GROUP, :]

    for q in range(pairs):
        zc = jnp.concatenate([zt[2 * q], zt[2 * q + 1]], axis=0)
        v = lax.dot_general(zc, bp_ref[q], (((0,), (0,)), ((), ())),
                            preferred_element_type=F32)
        for seg in range(S5_SEGS):
            src = slice(seg * seg_len, (seg + 1) * seg_len)
            dst = slice(seg * S5_SEG_PITCH, seg * S5_SEG_PITCH + seg_len)
            vbuf[q, dst, :] = v[src, 0:LANES]
            vbuf[pairs + q, dst, :] = v[src, LANES:]

    ar = jnp.broadcast_to(a_ref[0:1, :], (SUBLANES, lanes))
    ai = jnp.broadcast_to(a_ref[1:2, :], (SUBLANES, lanes))

    def advance(sr, si, i):
        rows = pl.ds(i, S5_SEGS, stride=S5_SEG_PITCH)
        xr = jnp.concatenate([vbuf[s, rows, :] for s in range(pairs)], axis=1)
        xi = jnp.concatenate([vbuf[pairs + s, rows, :] for s in range(pairs)], axis=1)
        return ar * sr - ai * si + xr, ar * si + ai * sr + xi

    def end_state(i, state):
        return advance(*state, i)

    zero = jnp.zeros((SUBLANES, lanes), F32)
    er, ei = lax.fori_loop(0, seg_len, end_state, (zero, zero), unroll=8)
    ebuf[:, re] = er
    ebuf[:, im] = ei

    atr = a_ref[2:3, :]
    ati = a_ref[3:4, :]
    cr = jnp.zeros((1, lanes), F32)
    ci = jnp.zeros((1, lanes), F32)
    for seg in range(S5_SEGS):
        sinbuf[seg:seg + 1, re] = cr
        sinbuf[seg:seg + 1, im] = ci
        cr, ci = (atr * cr - ati * ci + ebuf[seg:seg + 1, re],
                  atr * ci + ati * cr + ebuf[seg:seg + 1, im])

    def record(i, state):
        sr, si = state
        rows = pl.ds(i, S5_SEGS, stride=S5_SEG_PITCH)
        for s in range(pairs):
            spbuf[s, rows, :] = sr[:, s * LANES:(s + 1) * LANES]
            spbuf[pairs + s, rows, :] = si[:, s * LANES:(s + 1) * LANES]
        return advance(sr, si, i)

    lax.fori_loop(0, seg_len, record, (sinbuf[:, re], sinbuf[:, im]), unroll=8)

    def natural_rows(slab):
        return jnp.concatenate(
            [spbuf[slab, seg * S5_SEG_PITCH:seg * S5_SEG_PITCH + seg_len, :]
             for seg in range(S5_SEGS)], axis=0)

    for q in range(pairs):
        sp = jnp.concatenate([natural_rows(q), natural_rows(pairs + q)], axis=1)
        spt = sp.astype(BF16).T
        for gl in (2 * q, 2 * q + 1):
            rhs = jnp.concatenate([zt[gl], spt], axis=0)
            ytbuf[gl] = jnp.dot(wyt_ref[gl], rhs, preferred_element_type=F32)

    for j in range(S5_BLOCK):
        yt = jnp.concatenate([ytbuf[gl, j * S5_GROUP:(j + 1) * S5_GROUP, :]
                              for gl in range(S5_CHUNK_GROUPS)], axis=0)
        tokbuf[pl.ds(j, n_blk, stride=S5_BLOCK), :] = yt.T
    y_ref[...] = tokbuf[...].astype(BF16)


def _s5_params(lam_re, lam_im, log_dt, b_re, b_im, c_re, c_im, seg_len):
    hp = lax.Precision.HIGHEST
    r = S5_BLOCK
    g_n, p_n, h_n = S5_GROUPS, S5_STATE, S5_GROUP
    lr = jnp.minimum(lam_re.astype(F32), -1e-4)
    li = lam_im.astype(F32)
    dt = jnp.exp(log_dt.astype(F32))[:, None]
    mag = jnp.exp(lr * dt)
    ab_re = mag * jnp.cos(li * dt)
    ab_im = mag * jnp.sin(li * dt)
    den = lr * lr + li * li
    n_re = ab_re - 1.0
    n_im = ab_im
    z_re = (n_re * lr + n_im * li) / den
    z_im = (n_im * lr - n_re * li) / den
    br = b_re.astype(F32)
    bi = b_im.astype(F32)
    bb_re = z_re[..., None] * br - z_im[..., None] * bi
    bb_im = z_re[..., None] * bi + z_im[..., None] * br

    ldt = (lr * dt)[:, None, :]
    wdt = (li * dt)[:, None, :]

    def a_pow(k):
        pm = jnp.exp(ldt * k)
        return pm * jnp.cos(wdt * k), pm * jnp.sin(wdt * k)

    pr, pi = a_pow(jnp.arange(r + 1, dtype=F32)[None, :, None])
    prv = pr[:, r - 1::-1]
    piv = pi[:, r - 1::-1]
    cr = c_re.astype(F32)
    ci = c_im.astype(F32)

    car_re = cr[:, :, None, :] * prv[:, None] - ci[:, :, None, :] * piv[:, None]
    car_im = cr[:, :, None, :] * piv[:, None] + ci[:, :, None, :] * prv[:, None]
    kt_rev = (jnp.einsum('gamp,gph->gamh', car_re, bb_re, precision=hp)
              - jnp.einsum('gamp,gph->gamh', car_im, bb_im, precision=hp))
    strip = jnp.pad(kt_rev.reshape(g_n, h_n, r * h_n), ((0, 0), (0, 0), (0, (r - 1) * h_n)))
    mzt = jnp.stack([strip[:, :, (r - 1 - j) * h_n:(r - 1 - j) * h_n + r * h_n] for j in range(r)],
                    axis=1).reshape(g_n, r * h_n, r * h_n)

    cre = (cr[:, None] * pr[:, 1:, None, :] - ci[:, None] * pi[:, 1:, None, :]).reshape(
        g_n, r * h_n, p_n)
    cim = -(cr[:, None] * pi[:, 1:, None, :] + ci[:, None] * pr[:, 1:, None, :]).reshape(
        g_n, r * h_n, p_n)
    zc = jnp.zeros_like(cre)
    even = jnp.concatenate([cre, zc, cim, zc], axis=2)
    odd = jnp.concatenate([zc, cre, zc, cim], axis=2)
    is_odd = (jnp.arange(g_n) % 2 == 1)[:, None, None]
    wyt = jnp.concatenate([mzt, jnp.where(is_odd, odd, even)], axis=2).astype(BF16)

    bbr = jnp.transpose(bb_re, (0, 2, 1))[:, None]
    bbi = jnp.transpose(bb_im, (0, 2, 1))[:, None]
    prr = prv[:, :, None, :]
    pir = piv[:, :, None, :]
    bp_re = (prr * bbr - pir * bbi).reshape(g_n // 2, 2, r * h_n, p_n)
    bp_im = (prr * bbi + pir * bbr).reshape(g_n // 2, 2, r * h_n, p_n)
    zb = jnp.zeros_like(bp_re[:, 0])
    top = jnp.concatenate([bp_re[:, 0], zb, bp_im[:, 0], zb], axis=2)
    bot = jnp.concatenate([zb, bp_re[:, 1], zb, bp_im[:, 1]], axis=2)
    bp = jnp.concatenate([top, bot], axis=1).astype(BF16)

    at_re, at_im = a_pow(float(r * seg_len))
    n_chunks = g_n // S5_CHUNK_GROUPS
    a_tab = jnp.stack([x.reshape(n_chunks, S5_CHUNK_LANES)
                       for x in (pr[:, r], pi[:, r], at_re, at_im)], axis=1)
    return wyt, bp, a_tab


def _s5_branch(u, wyt, bp, a_tab, bsz, seq):
    n_blk = seq // S5_BLOCK
    seg_len = n_blk // S5_SEGS
    width = S5_BLOCK * S5_GROUP
    slab = S5_CHUNK_GROUPS * S5_GROUP
    scan_rows = S5_SEGS * S5_SEG_PITCH
    assert slab == LANES and S5_GROUPS // S5_CHUNK_GROUPS == S5_SLABS
    tok = pl.BlockSpec((None, seq, slab), lambda c, b: (c, b, 0))
    return pl.pallas_call(
        functools.partial(_s5_kernel, n_blk=n_blk, seg_len=seg_len),
        grid=(S5_GROUPS // S5_CHUNK_GROUPS, bsz),
        in_specs=[tok,
                  pl.BlockSpec((S5_CHUNK_GROUPS, width, 2 * width), lambda c, b: (c, 0, 0)),
                  pl.BlockSpec((S5_CHUNK_PAIRS, 2 * width, width), lambda c, b: (c, 0, 0)),
                  pl.BlockSpec((None, 4, S5_CHUNK_LANES), lambda c, b: (c, 0, 0))],
        out_specs=tok,
        out_shape=jax.ShapeDtypeStruct(u.shape, BF16),
        scratch_shapes=[
            pltpu.VMEM((seq, slab), F32),
            pltpu.VMEM((S5_CHUNK_GROUPS, width, n_blk), BF16),
            pltpu.VMEM((S5_CHUNK_GROUPS, width, n_blk), F32),
            pltpu.VMEM((2 * S5_CHUNK_PAIRS, scan_rows, LANES), F32),
            pltpu.VMEM((2 * S5_CHUNK_PAIRS, scan_rows, LANES), F32),
            pltpu.VMEM((S5_SEGS, 2 * S5_CHUNK_LANES), F32),
            pltpu.VMEM((S5_SEGS, 2 * S5_CHUNK_LANES), F32),
        ],
        compiler_params=pltpu.CompilerParams(
            dimension_semantics=("arbitrary", "arbitrary"), vmem_limit_bytes=VMEM_LIMIT_BYTES),
        name="s5_branch",
    )(u, wyt, bp, a_tab)


def _retention_kernel(q_ref, k_ref, kz_ref, v_ref, bz_ref, gain_ref, decay_ref, xi_ref,
                      yb_ref, state, *, chunk_decay):
    c = pl.program_id(1)

    @pl.when(c == 0)
    def _():
        state[...] = jnp.zeros_like(state)

    for h in range(RET_HEADS):
        cols = slice(h * RET_DK, (h + 1) * RET_DK)
        qh = q_ref[:, cols]
        vh = v_ref[:, cols]
        scores = lax.dot_general(qh, k_ref[:, cols], (((1,), (1,)), ((), ())),
                                 preferred_element_type=F32)
        inner = jnp.dot((scores * decay_ref[h]).astype(BF16), vh, preferred_element_type=F32)
        prev = state[h]
        cross = jnp.dot(qh, prev.astype(BF16), preferred_element_type=F32) * xi_ref[:, cols]
        local = lax.dot_general(kz_ref[:, cols], vh, (((0,), (0,)), ((), ())),
                                preferred_element_type=F32)
        state[h] = prev * chunk_decay[h] + local
        o = inner + cross
        mu = jnp.mean(o, axis=-1, keepdims=True)
        oc = o - mu
        var = jnp.mean(oc * oc, axis=-1, keepdims=True)
        o = oc * lax.rsqrt(var + NORM_EPS) * gain_ref[:, cols]
        yb_ref[:, cols] = (o * bz_ref[:, cols].astype(F32)).astype(BF16)


def _retention_tables():
    log_g = np.log1p(-np.exp2(-5.0 - np.arange(RET_HEADS, dtype=np.float64)))
    idx = np.arange(RET_TILE, dtype=np.float64)
    diff = idx[:, None] - idx[None, :]
    decay = np.where(diff >= 0, np.exp(log_g[:, None, None] * np.maximum(diff, 0.0)), 0.0)
    xi = np.repeat(np.exp(log_g[None, :] * (idx[:, None] + 1.0)), RET_DK, axis=1)
    zeta = np.repeat(np.exp(log_g[None, :] * (RET_TILE - 1.0 - idx[:, None])), RET_DK, axis=1)
    zeta = np.tile(zeta, (PROJ_TILE // RET_TILE, 1))
    chunk_decay = tuple(float(x) for x in np.exp(log_g * RET_TILE))
    return (jnp.asarray(decay, F32), jnp.asarray(xi, F32), jnp.asarray(zeta, F32), chunk_decay)


def _retention_branch(q, k, kz, v, bz, gain, decay, xi, chunk_decay, bsz, seq):
    chunks = seq // RET_TILE
    tok = pl.BlockSpec((RET_TILE, D_MODEL), lambda b, c: (b * chunks + c, 0))
    return pl.pallas_call(
        functools.partial(_retention_kernel, chunk_decay=chunk_decay),
        grid=(bsz, chunks),
        in_specs=[tok, tok, tok, tok, tok, _const_spec((1, D_MODEL)), _const_spec(decay.shape),
                  _const_spec(xi.shape)],
        out_specs=tok,
        out_shape=jax.ShapeDtypeStruct(q.shape, BF16),
        scratch_shapes=[pltpu.VMEM((RET_HEADS, RET_DK, RET_DV), F32)],
        compiler_params=pltpu.CompilerParams(
            dimension_semantics=("arbitrary", "arbitrary"), vmem_limit_bytes=VMEM_LIMIT_BYTES),
        name="retention_branch",
    )(q, k, kz, v, bz, gain, decay, xi)


def _even_out_kernel(x_ref, ys_ref, u_ref, az_ref, yb_ref, d_ref, wglu_ref, bglu_ref, w_ref,
                     o_ref):
    ys = jnp.concatenate([ys_ref[s] for s in range(S5_SLABS)], axis=1)
    u = jnp.concatenate([u_ref[s] for s in range(S5_SLABS)], axis=1)
    y = ys.astype(F32) + d_ref[...] * u.astype(F32)
    y = jax.nn.gelu(y)
    gl = jnp.dot(y.astype(BF16), wglu_ref[...], preferred_element_type=F32) + bglu_ref[...]
    ya = (y * jax.nn.sigmoid(gl) * az_ref[...].astype(F32)).astype(BF16)
    acc = jnp.dot(ya, w_ref[0:S5_WIDTH, :], preferred_element_type=F32)
    acc = acc + jnp.dot(yb_ref[...], w_ref[S5_WIDTH:, :], preferred_element_type=F32)
    o_ref[...] = x_ref[...] + acc


def _even_out_proj(x2d, ys, u, az, yb, d_skip, w_glu, b_glu, w_out):
    n = x2d.shape[0]
    tm = PROJ_TILE
    tok = pl.BlockSpec((tm, D_MODEL), lambda i: (i, 0))
    slabs = pl.BlockSpec((S5_SLABS, tm, LANES), lambda i: (0, i, 0))
    return pl.pallas_call(
        _even_out_kernel,
        grid=(n // tm,),
        in_specs=[tok, slabs, slabs, tok, tok, _const_spec((1, S5_WIDTH)), _const_spec(w_glu.shape),
                  _const_spec((1, S5_WIDTH)), _const_spec(w_out.shape)],
        out_specs=tok,
        out_shape=jax.ShapeDtypeStruct(x2d.shape, F32),
        compiler_params=pltpu.CompilerParams(
            dimension_semantics=("arbitrary",), vmem_limit_bytes=VMEM_LIMIT_BYTES),
        name="even_out_proj",
    )(x2d, ys, u, az, yb, d_skip, w_glu, b_glu, w_out)


def _odd_kernel(x_ref, g_ref, w_ref, vgain_ref, wsp_ref, bsp_ref, wout_ref, fg_ref, o_ref,
                vbuf, vnbuf, ybuf):
    x = x_ref[...]
    hn = _rms_norm(x, g_ref[...]).astype(BF16)
    gd = SGU_GROUP_DIM

    def proj(col):
        return jnp.dot(hn, w_ref[:, col:col + gd], preferred_element_type=F32)

    for g in range(SGU_GROUPS):
        vbuf[:, g * gd:(g + 1) * gd] = jax.nn.gelu(proj(SGU_WIDTH + g * gd))
    v = vbuf[...]
    mu = jnp.mean(v, axis=-1, keepdims=True)
    vc = v - mu
    var = jnp.mean(vc * vc, axis=-1, keepdims=True)
    vnbuf[...] = (vc * lax.rsqrt(var + NORM_EPS) * vgain_ref[...]).astype(BF16)

    row = lax.broadcasted_iota(jnp.int32, (SGU_CHUNK, SGU_CHUNK), 0)
    col = lax.broadcasted_iota(jnp.int32, (SGU_CHUNK, SGU_CHUNK), 1)
    for g in range(SGU_GROUPS):
        cols = slice(g * gd, (g + 1) * gd)
        ug = jax.nn.gelu(proj(g * gd))
        gate = _silu(proj(2 * SGU_WIDTH + g * gd))
        wm = jnp.where(row >= col, wsp_ref[g], 0.0).astype(BF16)
        bias = bsp_ref[g]
        for c in range(ODD_TILE // SGU_CHUNK):
            rows = slice(c * SGU_CHUNK, (c + 1) * SGU_CHUNK)
            s = jnp.dot(wm, vnbuf[rows, cols], preferred_element_type=F32) + bias
            ybuf[rows, cols] = (ug[rows] * s * gate[rows]).astype(BF16)

    xn = x + jnp.dot(ybuf[...], wout_ref[...], preferred_element_type=F32)
    o_ref[...] = _rms_norm(xn, fg_ref[...])


def _odd_layer(x2d, gain, w_in, vgain, wsp, bsp, w_out, final_gain):
    n = x2d.shape[0]
    tm = ODD_TILE
    tok = pl.BlockSpec((tm, D_MODEL), lambda i: (i, 0))
    return pl.pallas_call(
        _odd_kernel,
        grid=(n // tm,),
        in_specs=[tok, _const_spec((1, D_MODEL)), _const_spec(w_in.shape),
                  _const_spec((1, SGU_WIDTH)), _const_spec(wsp.shape), _const_spec(bsp.shape),
                  _const_spec(w_out.shape), _const_spec((1, D_MODEL))],
        out_specs=tok,
        out_shape=jax.ShapeDtypeStruct(x2d.shape, F32),
        scratch_shapes=[pltpu.VMEM((tm, SGU_WIDTH), F32),
                        pltpu.VMEM((tm, SGU_WIDTH), BF16),
                        pltpu.VMEM((tm, SGU_WIDTH), BF16)],
        compiler_params=pltpu.CompilerParams(
            dimension_semantics=("arbitrary",), vmem_limit_bytes=VMEM_LIMIT_BYTES),
        name="odd_layer",
    )(x2d, gain, w_in, vgain, wsp, bsp, w_out, final_gain)


def _rotary_tables(seq):
    half = RET_DK // 2
    pos = np.arange(seq, dtype=np.float64)
    inv = ROPE_BASE ** (-np.arange(half, dtype=np.float64) / half)
    ang = pos[:, None] * inv[None, :]
    return jnp.asarray(np.cos(ang), F32), jnp.asarray(np.sin(ang), F32)


def kernel(x, norm_even, w_in_even, s5_lam_re, s5_lam_im, s5_log_dt, s5_b_re, s5_b_im, s5_c_re, s5_c_im, s5_d, s5_w_glu, s5_b_glu, ret_gn_gain, w_out_even, norm_odd, w_in_odd, sgu_norm_gain, sgu_w_spatial, sgu_b_spatial, w_out_odd, final_norm):
    bsz, seq, d = x.shape
    x2d = x.reshape(bsz * seq, d)
    cos, sin = _rotary_tables(seq)

    decay, xi, zeta, chunk_decay = _retention_tables()
    u, az, q, k, kz, v, bz = _even_in_proj(
        x2d, norm_even[0].reshape(1, d), w_in_even[0].astype(BF16), cos, sin, zeta, seq)
    seg_len = seq // S5_BLOCK // S5_SEGS
    wyt, bp, a_tab = _s5_params(s5_lam_re[0], s5_lam_im[0], s5_log_dt[0], s5_b_re[0], s5_b_im[0],
                                s5_c_re[0], s5_c_im[0], seg_len)
    ys = _s5_branch(u, wyt, bp, a_tab, bsz, seq)
    yb = _retention_branch(q, k, kz, v, bz, ret_gn_gain[0].reshape(1, -1).astype(F32),
                           decay, xi, chunk_decay, bsz, seq)
    x2d = _even_out_proj(x2d, ys, u, az, yb, s5_d[0].reshape(1, -1).astype(F32),
                         s5_w_glu[0].astype(BF16), s5_b_glu[0].reshape(1, -1).astype(F32),
                         w_out_even[0].astype(BF16))

    out = _odd_layer(x2d, norm_odd[0].reshape(1, d), w_in_odd[0].astype(BF16),
                     sgu_norm_gain[0].reshape(1, -1).astype(F32), sgu_w_spatial[0].astype(F32),
                     sgu_b_spatial[0].astype(F32)[:, :, None], w_out_odd[0].astype(BF16),
                     final_norm.reshape(1, d))
    return out.reshape(bsz, seq, d)
```

```python
import functools

import jax
import jax.numpy as jnp
import numpy as np
from jax import lax
from jax.experimental import pallas as pl
from jax.experimental.pallas import tpu as pltpu

F32 = jnp.float32
BF16 = jnp.bfloat16

D_MODEL = 1024
S5_WIDTH = 1024
S5_GROUP = 16
S5_GROUPS = 64
S5_STATE = 64
RET_HEADS = 4
RET_DK = 256
RET_DV = 256
ROPE_BASE = 10000.0
SGU_WIDTH = 2048
SGU_GROUPS = 4
SGU_GROUP_DIM = 512
SGU_CHUNK = 128
NORM_EPS = 1e-6

V7X_VMEM_BYTES = 64 * 1024 * 1024
VMEM_LIMIT_BYTES = V7X_VMEM_BYTES - 12 * 1024 * 1024
SUBLANES = 8
LANES = 128

PROJ_TILE = 512
OUT_TILE = 1024
S5_BLOCK = 16
S5_SEGS = SUBLANES
S5_CHUNK_GROUPS = 8
S5_CHUNK_PAIRS = S5_CHUNK_GROUPS // 2
S5_CHUNK_LANES = S5_CHUNK_GROUPS * S5_STATE
S5_SLABS = S5_WIDTH // LANES
S5_SEG_PITCH = 72
RET_CHUNK = 256
RET_STEP_CHUNKS = 2
ODD_TILE = 512


def _const_spec(shape):
    zeros = (0,) * len(shape)
    return pl.BlockSpec(shape, lambda *_: zeros, pipeline_mode=pl.Buffered(1))


def _rms_norm(x, gain):
    ms = jnp.mean(x * x, axis=-1, keepdims=True)
    return x * lax.rsqrt(ms + NORM_EPS) * gain


def _silu(x):
    return x * jax.nn.sigmoid(x)


def _even_in_kernel(x_ref, g_ref, w_ref, cos_ref, sin_ref, zeta_ref,
                    u_ref, az_ref, q_ref, k_ref, kz_ref, v_ref, bz_ref):
    hn = _rms_norm(x_ref[...], g_ref[...]).astype(BF16)

    def proj(j):
        return jnp.dot(hn, w_ref[:, j * D_MODEL:(j + 1) * D_MODEL], preferred_element_type=F32)

    def rotary_store(p, out_ref, scale, decayed_ref=None):
        cos = cos_ref[...]
        sin = sin_ref[...]
        half = RET_DK // 2
        for h in range(RET_HEADS):
            lo = slice(h * RET_DK, h * RET_DK + half)
            hi = slice(h * RET_DK + half, (h + 1) * RET_DK)
            x1 = p[:, lo]
            x2 = p[:, hi]
            o1 = x1 * cos - x2 * sin
            o2 = x1 * sin + x2 * cos
            if scale != 1.0:
                o1 = o1 * scale
                o2 = o2 * scale
            out_ref[:, lo] = o1.astype(BF16)
            out_ref[:, hi] = o2.astype(BF16)
            if decayed_ref is not None:
                decayed_ref[:, lo] = (o1 * zeta_ref[:, lo]).astype(BF16)
                decayed_ref[:, hi] = (o2 * zeta_ref[:, hi]).astype(BF16)

    u = proj(0).astype(BF16)
    for s in range(S5_SLABS):
        u_ref[s] = u[:, s * LANES:(s + 1) * LANES]
    az_ref[...] = _silu(proj(1)).astype(BF16)
    rotary_store(proj(2), q_ref, 1.0)
    rotary_store(proj(3), k_ref, RET_DK ** -0.5, kz_ref)
    v_ref[...] = proj(4).astype(BF16)
    bz_ref[...] = _silu(proj(5)).astype(BF16)


def _even_in_proj(x2d, gain, w_in, cos, sin, zeta, seq):
    n = x2d.shape[0]
    tm = PROJ_TILE
    tiles_per_seq = seq // tm
    tok = pl.BlockSpec((tm, D_MODEL), lambda i: (i, 0))
    rot = pl.BlockSpec((tm, RET_DK // 2), lambda i: (i % tiles_per_seq, 0))
    out = jax.ShapeDtypeStruct((n, D_MODEL), BF16)
    return pl.pallas_call(
        _even_in_kernel,
        grid=(n // tm,),
        in_specs=[tok, _const_spec((1, D_MODEL)), _const_spec(w_in.shape), rot, rot,
                  _const_spec(zeta.shape)],
        out_specs=[pl.BlockSpec((S5_SLABS, tm, LANES), lambda i: (0, i, 0))] + [tok] * 6,
        out_shape=[jax.ShapeDtypeStruct((S5_SLABS, n, LANES), BF16)] + [out] * 6,
        compiler_params=pltpu.CompilerParams(
            dimension_semantics=("arbitrary",), vmem_limit_bytes=VMEM_LIMIT_BYTES),
        name="even_in_proj",
    )(x2d, gain, w_in, cos, sin, zeta)


def _s5_kernel(u_ref, wyt_ref, bp_ref, a_ref, y_ref, tokbuf, zt, ytbuf, vbuf, spbuf, ebuf, sinbuf,
               *, n_blk, seg_len):
    lanes = S5_CHUNK_LANES
    re = slice(0, lanes)
    im = slice(lanes, 2 * lanes)
    pairs = S5_CHUNK_PAIRS

    tokbuf[...] = u_ref[...].astype(F32)
    for j in range(S5_BLOCK):
        ujt = tokbuf[pl.ds(j, n_blk, stride=S5_BLOCK), :].T.astype(BF16)
        for gl in range(S5_CHUNK_GROUPS):
            zt[gl, j * S5_GROUP:(j + 1) * S5_GROUP, :] = ujt[gl * S5_GROUP:(gl + 1) * S5_GROUP, :]

    for q in range(pairs):
        zc = jnp.concatenate([zt[2 * q], zt[2 * q + 1]], axis=0)
        v = lax.dot_general(zc, bp_ref[q], (((0,), (0,)), ((), ())),
                            preferred_element_type=F32)
        for seg in range(S5_SEGS):
            src = slice(seg * seg_len, (seg + 1) * seg_len)
            dst = slice(seg * S5_SEG_PITCH, seg * S5_SEG_PITCH + seg_len)
            vbuf[q, dst, :] = v[src, 0:LANES]
            vbuf[pairs + q, dst, :] = v[src, LANES:]

    ar = jnp.broadcast_to(a_ref[0:1, :], (SUBLANES, lanes))
    ai = jnp.broadcast_to(a_ref[1:2, :], (SUBLANES, lanes))

    def advance(sr, si, i):
        rows = pl.ds(i, S5_SEGS, stride=S5_SEG_PITCH)
        xr = jnp.concatenate([vbuf[s, rows, :] for s in range(pairs)], axis=1)
        xi = jnp.concatenate([vbuf[pairs + s, rows, :] for s in range(pairs)], axis=1)
        return ar * sr - ai * si + xr, ar * si + ai * sr + xi

    def end_state(i, state):
        return advance(*state, i)

    zero = jnp.zeros((SUBLANES, lanes), F32)
    er, ei = lax.fori_loop(0, seg_len, end_state, (zero, zero), unroll=8)
    ebuf[:, re] = er
    ebuf[:, im] = ei

    atr = a_ref[2:3, :]
    ati = a_ref[3:4, :]
    cr = jnp.zeros((1, lanes), F32)
    ci = jnp.zeros((1, lanes), F32)
    for seg in range(S5_SEGS):
        sinbuf[seg:seg + 1, re] = cr
        sinbuf[seg:seg + 1, im] = ci
        cr, ci = (atr * cr - ati * ci + ebuf[seg:seg + 1, re],
                  atr * ci + ati * cr + ebuf[seg:seg + 1, im])

    def record(i, state):
        sr, si = state
        rows = pl.ds(i, S5_SEGS, stride=S5_SEG_PITCH)
        for s in range(pairs):
            spbuf[s, rows, :] = sr[:, s * LANES:(s + 1) * LANES]
            spbuf[pairs + s, rows, :] = si[:, s * LANES:(s + 1) * LANES]
        return advance(sr, si, i)

    lax.fori_loop(0, seg_len, record, (sinbuf[:, re], sinbuf[:, im]), unroll=8)

    def natural_rows(slab):
        return jnp.concatenate(
            [spbuf[slab, seg * S5_SEG_PITCH:seg * S5_SEG_PITCH + seg_len, :]
             for seg in range(S5_SEGS)], axis=0)

    for q in range(pairs):
        sp = jnp.concatenate([natural_rows(q), natural_rows(pairs + q)], axis=1)
        spt = sp.astype(BF16).T
        for gl in (2 * q, 2 * q + 1):
            rhs = jnp.concatenate([zt[gl], spt], axis=0)
            ytbuf[gl] = jnp.dot(wyt_ref[gl], rhs, preferred_element_type=F32)

    for j in range(S5_BLOCK):
        yt = jnp.concatenate([ytbuf[gl, j * S5_GROUP:(j + 1) * S5_GROUP, :]
                              for gl in range(S5_CHUNK_GROUPS)], axis=0)
        tokbuf[pl.ds(j, n_blk, stride=S5_BLOCK), :] = yt.T
    y_ref[...] = tokbuf[...].astype(BF16)


def _s5_params(lam_re, lam_im, log_dt, b_re, b_im, c_re, c_im, seg_len):
    hp = lax.Precision.HIGHEST
    r = S5_BLOCK
    g_n, p_n, h_n = S5_GROUPS, S5_STATE, S5_GROUP
    lr = jnp.minimum(lam_re.astype(F32), -1e-4)
    li = lam_im.astype(F32)
    dt = jnp.exp(log_dt.astype(F32))[:, None]
    mag = jnp.exp(lr * dt)
    ab_re = mag * jnp.cos(li * dt)
    ab_im = mag * jnp.sin(li * dt)
    den = lr * lr + li * li
    n_re = ab_re - 1.0
    n_im = ab_im
    z_re = (n_re * lr + n_im * li) / den
    z_im = (n_im * lr - n_re * li) / den
    br = b_re.astype(F32)
    bi = b_im.astype(F32)
    bb_re = z_re[..., None] * br - z_im[..., None] * bi
    bb_im = z_re[..., None] * bi + z_im[..., None] * br

    ldt = (lr * dt)[:, None, :]
    wdt = (li * dt)[:, None, :]

    def a_pow(k):
        pm = jnp.exp(ldt * k)
        return pm * jnp.cos(wdt * k), pm * jnp.sin(wdt * k)

    pr, pi = a_pow(jnp.arange(r + 1, dtype=F32)[None, :, None])
    prv = pr[:, r - 1::-1]
    piv = pi[:, r - 1::-1]
    cr = c_re.astype(F32)
    ci = c_im.astype(F32)

    car_re = cr[:, :, None, :] * prv[:, None] - ci[:, :, None, :] * piv[:, None]
    car_im = cr[:, :, None, :] * piv[:, None] + ci[:, :, None, :] * prv[:, None]
    kt_rev = (jnp.einsum('gamp,gph->gamh', car_re, bb_re, precision=hp)
              - jnp.einsum('gamp,gph->gamh', car_im, bb_im, precision=hp))
    strip = jnp.pad(kt_rev.reshape(g_n, h_n, r * h_n), ((0, 0), (0, 0), (0, (r - 1) * h_n)))
    mzt = jnp.stack([strip[:, :, (r - 1 - j) * h_n:(r - 1 - j) * h_n + r * h_n] for j in range(r)],
                    axis=1).reshape(g_n, r * h_n, r * h_n)

    cre = (cr[:, None] * pr[:, 1:, None, :] - ci[:, None] * pi[:, 1:, None, :]).reshape(
        g_n, r * h_n, p_n)
    cim = -(cr[:, None] * pi[:, 1:, None, :] + ci[:, None] * pr[:, 1:, None, :]).reshape(
        g_n, r * h_n, p_n)
    zc = jnp.zeros_like(cre)
    even = jnp.concatenate([cre, zc, cim, zc], axis=2)
    odd = jnp.concatenate([zc, cre, zc, cim], axis=2)
    is_odd = (jnp.arange(g_n) % 2 == 1)[:, None, None]
    wyt = jnp.concatenate([mzt, jnp.where(is_odd, odd, even)], axis=2).astype(BF16)

    bbr = jnp.transpose(bb_re, (0, 2, 1))[:, None]
    bbi = jnp.transpose(bb_im, (0, 2, 1))[:, None]
    prr = prv[:, :, None, :]
    pir = piv[:, :, None, :]
    bp_re = (prr * bbr - pir * bbi).reshape(g_n // 2, 2, r * h_n, p_n)
    bp_im = (prr * bbi + pir * bbr).reshape(g_n // 2, 2, r * h_n, p_n)
    zb = jnp.zeros_like(bp_re[:, 0])
    top = jnp.concatenate([bp_re[:, 0], zb, bp_im[:, 0], zb], axis=2)
    bot = jnp.concatenate([zb, bp_re[:, 1], zb, bp_im[:, 1]], axis=2)
    bp = jnp.concatenate([top, bot], axis=1).astype(BF16)

    at_re, at_im = a_pow(float(r * seg_len))
    n_chunks = g_n // S5_CHUNK_GROUPS
    a_tab = jnp.stack([x.reshape(n_chunks, S5_CHUNK_LANES)
                       for x in (pr[:, r], pi[:, r], at_re, at_im)], axis=1)
    return wyt, bp, a_tab


def _s5_branch(u, wyt, bp, a_tab, bsz, seq):
    n_blk = seq // S5_BLOCK
    seg_len = n_blk // S5_SEGS
    width = S5_BLOCK * S5_GROUP
    slab = S5_CHUNK_GROUPS * S5_GROUP
    scan_rows = S5_SEGS * S5_SEG_PITCH
    assert slab == LANES and S5_GROUPS // S5_CHUNK_GROUPS == S5_SLABS
    tok = pl.BlockSpec((None, seq, slab), lambda c, b: (c, b, 0))
    return pl.pallas_call(
        functools.partial(_s5_kernel, n_blk=n_blk, seg_len=seg_len),
        grid=(S5_GROUPS // S5_CHUNK_GROUPS, bsz),
        in_specs=[tok,
                  pl.BlockSpec((S5_CHUNK_GROUPS, width, 2 * width), lambda c, b: (c, 0, 0)),
                  pl.BlockSpec((S5_CHUNK_PAIRS, 2 * width, width), lambda c, b: (c, 0, 0)),
                  pl.BlockSpec((None, 4, S5_CHUNK_LANES), lambda c, b: (c, 0, 0))],
        out_specs=tok,
        out_shape=jax.ShapeDtypeStruct(u.shape, BF16),
        scratch_shapes=[
            pltpu.VMEM((seq, slab), F32),
            pltpu.VMEM((S5_CHUNK_GROUPS, width, n_blk), BF16),
            pltpu.VMEM((S5_CHUNK_GROUPS, width, n_blk), F32),
            pltpu.VMEM((2 * S5_CHUNK_PAIRS, scan_rows, LANES), F32),
            pltpu.VMEM((2 * S5_CHUNK_PAIRS, scan_rows, LANES), F32),
            pltpu.VMEM((S5_SEGS, 2 * S5_CHUNK_LANES), F32),
            pltpu.VMEM((S5_SEGS, 2 * S5_CHUNK_LANES), F32),
        ],
        compiler_params=pltpu.CompilerParams(
            dimension_semantics=("arbitrary", "arbitrary"), vmem_limit_bytes=VMEM_LIMIT_BYTES),
        name="s5_branch",
    )(u, wyt, bp, a_tab)


def _retention_kernel(q_ref, k_ref, kz_ref, v_ref, bz_ref, gain_ref, decay_ref, xi_ref,
                      yb_ref, state, *, chunk_decay):
    c = pl.program_id(1)

    @pl.when(c == 0)
    def _():
        state[...] = jnp.zeros_like(state)

    for ck in range(RET_STEP_CHUNKS):
        rows = slice(ck * RET_CHUNK, (ck + 1) * RET_CHUNK)
        for h in range(RET_HEADS):
            cols = slice(h * RET_DK, (h + 1) * RET_DK)
            qh = q_ref[rows, cols]
            vh = v_ref[rows, cols]
            scores = lax.dot_general(qh, k_ref[rows, cols], (((1,), (1,)), ((), ())),
                                     preferred_element_type=F32)
            inner = jnp.dot((scores * decay_ref[h]).astype(BF16), vh, preferred_element_type=F32)
            prev = state[h]
            cross = jnp.dot(qh, prev.astype(BF16), preferred_element_type=F32) * xi_ref[:, cols]
            local = lax.dot_general(kz_ref[rows, cols], vh, (((0,), (0,)), ((), ())),
                                    preferred_element_type=F32)
            state[h] = prev * chunk_decay[h] + local
            o = inner + cross
            mu = jnp.mean(o, axis=-1, keepdims=True)
            oc = o - mu
            var = jnp.mean(oc * oc, axis=-1, keepdims=True)
            o = oc * lax.rsqrt(var + NORM_EPS) * gain_ref[:, cols]
            yb_ref[rows, cols] = (o * bz_ref[rows, cols].astype(F32)).astype(BF16)


def _retention_tables():
    log_g = np.log1p(-np.exp2(-5.0 - np.arange(RET_HEADS, dtype=np.float64)))
    idx = np.arange(RET_CHUNK, dtype=np.float64)
    diff = idx[:, None] - idx[None, :]
    decay = np.where(diff >= 0, np.exp(log_g[:, None, None] * np.maximum(diff, 0.0)), 0.0)
    xi = np.repeat(np.exp(log_g[None, :] * (idx[:, None] + 1.0)), RET_DK, axis=1)
    zeta = np.repeat(np.exp(log_g[None, :] * (RET_CHUNK - 1.0 - idx[:, None])), RET_DK, axis=1)
    zeta = np.tile(zeta, (PROJ_TILE // RET_CHUNK, 1))
    chunk_decay = tuple(float(x) for x in np.exp(log_g * RET_CHUNK))
    return (jnp.asarray(decay, F32), jnp.asarray(xi, F32), jnp.asarray(zeta, F32), chunk_decay)


def _retention_branch(q, k, kz, v, bz, gain, decay, xi, chunk_decay, bsz, seq):
    step = RET_STEP_CHUNKS * RET_CHUNK
    steps = seq // step
    tok = pl.BlockSpec((step, D_MODEL), lambda b, c: (b * steps + c, 0))
    return pl.pallas_call(
        functools.partial(_retention_kernel, chunk_decay=chunk_decay),
        grid=(bsz, steps),
        in_specs=[tok, tok, tok, tok, tok, _const_spec((1, D_MODEL)), _const_spec(decay.shape),
                  _const_spec(xi.shape)],
        out_specs=tok,
        out_shape=jax.ShapeDtypeStruct(q.shape, BF16),
        scratch_shapes=[pltpu.VMEM((RET_HEADS, RET_DK, RET_DV), F32)],
        compiler_params=pltpu.CompilerParams(
            dimension_semantics=("arbitrary", "arbitrary"), vmem_limit_bytes=VMEM_LIMIT_BYTES),
        name="retention_branch",
    )(q, k, kz, v, bz, gain, decay, xi)


def _even_out_kernel(x_ref, ys_ref, u_ref, az_ref, yb_ref, d_ref, wglu_ref, bglu_ref, w_ref,
                     o_ref):
    ys = jnp.concatenate([ys_ref[s] for s in range(S5_SLABS)], axis=1)
    u = jnp.concatenate([u_ref[s] for s in range(S5_SLABS)], axis=1)
    y = ys.astype(F32) + d_ref[...] * u.astype(F32)
    y = jax.nn.gelu(y)
    gl = jnp.dot(y.astype(BF16), wglu_ref[...], preferred_element_type=F32) + bglu_ref[...]
    ya = (y * jax.nn.sigmoid(gl) * az_ref[...].astype(F32)).astype(BF16)
    acc = jnp.dot(ya, w_ref[0:S5_WIDTH, :], preferred_element_type=F32)
    acc = acc + jnp.dot(yb_ref[...], w_ref[S5_WIDTH:, :], preferred_element_type=F32)
    o_ref[...] = x_ref[...] + acc


def _even_out_proj(x2d, ys, u, az, yb, d_skip, w_glu, b_glu, w_out):
    n = x2d.shape[0]
    tm = OUT_TILE
    tok = pl.BlockSpec((tm, D_MODEL), lambda i: (i, 0))
    slabs = pl.BlockSpec((S5_SLABS, tm, LANES), lambda i: (0, i, 0))
    return pl.pallas_call(
        _even_out_kernel,
        grid=(n // tm,),
        in_specs=[tok, slabs, slabs, tok, tok, _const_spec((1, S5_WIDTH)), _const_spec(w_glu.shape),
                  _const_spec((1, S5_WIDTH)), _const_spec(w_out.shape)],
        out_specs=tok,
        out_shape=jax.ShapeDtypeStruct(x2d.shape, F32),
        compiler_params=pltpu.CompilerParams(
            dimension_semantics=("arbitrary",), vmem_limit_bytes=VMEM_LIMIT_BYTES),
        name="even_out_proj",
    )(x2d, ys, u, az, yb, d_skip, w_glu, b_glu, w_out)


def _odd_kernel(x_ref, g_ref, w_ref, vgain_ref, wsp_ref, bsp_ref, wout_ref, fg_ref, o_ref,
                vbuf, ubuf, gbuf, vnbuf, ybuf):
    x = x_ref[...]
    hn = _rms_norm(x, g_ref[...]).astype(BF16)
    gd = SGU_GROUP_DIM

    def proj(col):
        return jnp.dot(hn, w_ref[:, col:col + gd], preferred_element_type=F32)

    for g in range(SGU_GROUPS):
        vbuf[:, g * gd:(g + 1) * gd] = jax.nn.gelu(proj(SGU_WIDTH + g * gd))
    for g in range(SGU_GROUPS):
        ubuf[:, g * gd:(g + 1) * gd] = jax.nn.gelu(proj(g * gd))
        gbuf[:, g * gd:(g + 1) * gd] = _silu(proj(2 * SGU_WIDTH + g * gd))
    v = vbuf[...]
    mu = jnp.mean(v, axis=-1, keepdims=True)
    vc = v - mu
    var = jnp.mean(vc * vc, axis=-1, keepdims=True)
    vnbuf[...] = (vc * lax.rsqrt(var + NORM_EPS) * vgain_ref[...]).astype(BF16)

    row = lax.broadcasted_iota(jnp.int32, (SGU_CHUNK, SGU_CHUNK), 0)
    col = lax.broadcasted_iota(jnp.int32, (SGU_CHUNK, SGU_CHUNK), 1)
    for g in range(SGU_GROUPS):
        cols = slice(g * gd, (g + 1) * gd)
        wm = jnp.where(row >= col, wsp_ref[g], 0.0).astype(BF16)
        bias = bsp_ref[g]
        for c in range(ODD_TILE // SGU_CHUNK):
            rows = slice(c * SGU_CHUNK, (c + 1) * SGU_CHUNK)
            s = jnp.dot(wm, vnbuf[rows, cols], preferred_element_type=F32) + bias
            ybuf[rows, cols] = (ubuf[rows, cols] * s * gbuf[rows, cols]).astype(BF16)

    xn = x + jnp.dot(ybuf[...], wout_ref[...], preferred_element_type=F32)
    o_ref[...] = _rms_norm(xn, fg_ref[...])


def _odd_layer(x2d, gain, w_in, vgain, wsp, bsp, w_out, final_gain):
    n = x2d.shape[0]
    tm = ODD_TILE
    tok = pl.BlockSpec((tm, D_MODEL), lambda i: (i, 0))
    return pl.pallas_call(
        _odd_kernel,
        grid=(n // tm,),
        in_specs=[tok, _const_spec((1, D_MODEL)), _const_spec(w_in.shape),
                  _const_spec((1, SGU_WIDTH)), _const_spec(wsp.shape), _const_spec(bsp.shape),
                  _const_spec(w_out.shape), _const_spec((1, D_MODEL))],
        out_specs=tok,
        out_shape=jax.ShapeDtypeStruct(x2d.shape, F32),
        scratch_shapes=[pltpu.VMEM((tm, SGU_WIDTH), F32),
                        pltpu.VMEM((tm, SGU_WIDTH), F32),
                        pltpu.VMEM((tm, SGU_WIDTH), F32),
                        pltpu.VMEM((tm, SGU_WIDTH), BF16),
                        pltpu.VMEM((tm, SGU_WIDTH), BF16)],
        compiler_params=pltpu.CompilerParams(
            dimension_semantics=("arbitrary",), vmem_limit_bytes=VMEM_LIMIT_BYTES),
        name="odd_layer",
    )(x2d, gain, w_in, vgain, wsp, bsp, w_out, final_gain)


def _rotary_tables(seq):
    half = RET_DK // 2
    pos = np.arange(seq, dtype=np.float64)
    inv = ROPE_BASE ** (-np.arange(half, dtype=np.float64) / half)
    ang = pos[:, None] * inv[None, :]
    return jnp.asarray(np.cos(ang), F32), jnp.asarray(np.sin(ang), F32)


def kernel(x, norm_even, w_in_even, s5_lam_re, s5_lam_im, s5_log_dt, s5_b_re, s5_b_im, s5_c_re, s5_c_im, s5_d, s5_w_glu, s5_b_glu, ret_gn_gain, w_out_even, norm_odd, w_in_odd, sgu_norm_gain, sgu_w_spatial, sgu_b_spatial, w_out_odd, final_norm):
    bsz, seq, d = x.shape
    x2d = x.reshape(bsz * seq, d)
    cos, sin = _rotary_tables(seq)

    decay, xi, zeta, chunk_decay = _retention_tables()
    u, az, q, k, kz, v, bz = _even_in_proj(
        x2d, norm_even[0].reshape(1, d), w_in_even[0].astype(BF16), cos, sin, zeta, seq)
    seg_len = seq // S5_BLOCK // S5_SEGS
    wyt, bp, a_tab = _s5_params(s5_lam_re[0], s5_lam_im[0], s5_log_dt[0], s5_b_re[0], s5_b_im[0],
                                s5_c_re[0], s5_c_im[0], seg_len)
    ys = _s5_branch(u, wyt, bp, a_tab, bsz, seq)
    yb = _retention_branch(q, k, kz, v, bz, ret_gn_gain[0].reshape(1, -1).astype(F32),
                           decay, xi, chunk_decay, bsz, seq)
    x2d = _even_out_proj(x2d, ys, u, az, yb, s5_d[0].reshape(1, -1).astype(F32),
                         s5_w_glu[0].astype(BF16), s5_b_glu[0].reshape(1, -1).astype(F32),
                         w_out_even[0].astype(BF16))

    out = _odd_layer(x2d, norm_odd[0].reshape(1, d), w_in_odd[0].astype(BF16),
                     sgu_norm_gain[0].reshape(1, -1).astype(F32), sgu_w_spatial[0].astype(F32),
                     sgu_b_spatial[0].astype(F32)[:, :, None], w_out_odd[0].astype(BF16),
                     final_norm.reshape(1, d))
    return out.reshape(bsz, seq, d)
```

```python
import functools

import jax
import jax.numpy as jnp
import numpy as np
from jax import lax
from jax.experimental import pallas as pl
from jax.experimental.pallas import tpu as pltpu

F32 = jnp.float32
BF16 = jnp.bfloat16

D_MODEL = 1024
S5_WIDTH = 1024
S5_GROUP = 16
S5_GROUPS = 64
S5_STATE = 64
RET_HEADS = 4
RET_DK = 256
RET_DV = 256
ROPE_BASE = 10000.0
SGU_WIDTH = 2048
SGU_GROUPS = 4
SGU_GROUP_DIM = 512
SGU_CHUNK = 128
NORM_EPS = 1e-6

V7X_VMEM_BYTES = 64 * 1024 * 1024
VMEM_LIMIT_BYTES = V7X_VMEM_BYTES - 12 * 1024 * 1024
SUBLANES = 8
LANES = 128

PROJ_TILE = 512
OUT_TILE = 512
S5_BLOCK = 16
S5_SEGS = SUBLANES
S5_CHUNK_GROUPS = 8
S5_CHUNK_PAIRS = S5_CHUNK_GROUPS // 2
S5_CHUNK_LANES = S5_CHUNK_GROUPS * S5_STATE
S5_SLABS = S5_WIDTH // LANES
S5_SEG_PITCH = 72
RET_CHUNK = 256
RET_STEP_CHUNKS = 4
ODD_TILE = 512


def _const_spec(shape, single_buffer=True):
    zeros = (0,) * len(shape)
    if single_buffer:
        return pl.BlockSpec(shape, lambda *_: zeros, pipeline_mode=pl.Buffered(1))
    return pl.BlockSpec(shape, lambda *_: zeros)


def _rms_norm(x, gain):
    ms = jnp.mean(x * x, axis=-1, keepdims=True)
    return x * lax.rsqrt(ms + NORM_EPS) * gain


def _silu(x):
    return x * jax.nn.sigmoid(x)


def _even_in_kernel(x_ref, g_ref, w_ref, cos_ref, sin_ref, zeta_ref,
                    u_ref, az_ref, q_ref, k_ref, kz_ref, v_ref, bz_ref):
    hn = _rms_norm(x_ref[...], g_ref[...]).astype(BF16)

    def proj(j):
        return jnp.dot(hn, w_ref[:, j * D_MODEL:(j + 1) * D_MODEL], preferred_element_type=F32)

    def rotary_store(p, out_ref, scale, decayed_ref=None):
        cos = cos_ref[...]
        sin = sin_ref[...]
        half = RET_DK // 2
        for h in range(RET_HEADS):
            lo = slice(h * RET_DK, h * RET_DK + half)
            hi = slice(h * RET_DK + half, (h + 1) * RET_DK)
            x1 = p[:, lo]
            x2 = p[:, hi]
            o1 = x1 * cos - x2 * sin
            o2 = x1 * sin + x2 * cos
            if scale != 1.0:
                o1 = o1 * scale
                o2 = o2 * scale
            out_ref[:, lo] = o1.astype(BF16)
            out_ref[:, hi] = o2.astype(BF16)
            if decayed_ref is not None:
                decayed_ref[:, lo] = (o1 * zeta_ref[:, lo]).astype(BF16)
                decayed_ref[:, hi] = (o2 * zeta_ref[:, hi]).astype(BF16)

    u = proj(0).astype(BF16)
    for s in range(S5_SLABS):
        u_ref[s] = u[:, s * LANES:(s + 1) * LANES]
    az_ref[...] = _silu(proj(1)).astype(BF16)
    rotary_store(proj(2), q_ref, 1.0)
    rotary_store(proj(3), k_ref, RET_DK ** -0.5, kz_ref)
    v_ref[...] = proj(4).astype(BF16)
    bz_ref[...] = _silu(proj(5)).astype(BF16)


def _even_in_proj(x2d, gain, w_in, cos, sin, zeta, seq):
    n = x2d.shape[0]
    tm = PROJ_TILE
    tiles_per_seq = seq // tm
    tok = pl.BlockSpec((tm, D_MODEL), lambda i: (i, 0))
    rot = pl.BlockSpec((tm, RET_DK // 2), lambda i: (i % tiles_per_seq, 0))
    out = jax.ShapeDtypeStruct((n, D_MODEL), BF16)
    return pl.pallas_call(
        _even_in_kernel,
        grid=(n // tm,),
        in_specs=[tok, _const_spec((1, D_MODEL)), _const_spec(w_in.shape), rot, rot,
                  _const_spec(zeta.shape)],
        out_specs=[pl.BlockSpec((S5_SLABS, tm, LANES), lambda i: (0, i, 0))] + [tok] * 6,
        out_shape=[jax.ShapeDtypeStruct((S5_SLABS, n, LANES), BF16)] + [out] * 6,
        compiler_params=pltpu.CompilerParams(
            dimension_semantics=("arbitrary",), vmem_limit_bytes=VMEM_LIMIT_BYTES),
        name="even_in_proj",
    )(x2d, gain, w_in, cos, sin, zeta)


def _s5_kernel(u_ref, wyt_ref, bp_ref, a_ref, y_ref, tokbuf, zt, ytbuf, vbuf, spbuf, ebuf, sinbuf,
               *, n_blk, seg_len):
    lanes = S5_CHUNK_LANES
    re = slice(0, lanes)
    im = slice(lanes, 2 * lanes)
    pairs = S5_CHUNK_PAIRS

    tokbuf[...] = u_ref[...].astype(F32)
    for j in range(S5_BLOCK):
        ujt = tokbuf[pl.ds(j, n_blk, stride=S5_BLOCK), :].astype(BF16).T
        for gl in range(S5_CHUNK_GROUPS):
            zt[gl, j * S5_GROUP:(j + 1) * S5_GROUP, :] = ujt[gl * S5_GROUP:(gl + 1) * S5_GROUP, :]

    for q in range(pairs):
        zc = jnp.concatenate([zt[2 * q], zt[2 * q + 1]], axis=0)
        v = lax.dot_general(zc, bp_ref[q], (((0,), (0,)), ((), ())),
                            preferred_element_type=F32)
        for seg in range(S5_SEGS):
            src = slice(seg * seg_len, (seg + 1) * seg_len)
            dst = slice(seg * S5_SEG_PITCH, seg * S5_SEG_PITCH + seg_len)
            vbuf[q, dst, :] = v[src, 0:LANES]
            vbuf[pairs + q, dst, :] = v[src, LANES:]

    ar = jnp.broadcast_to(a_ref[0:1, :], (SUBLANES, lanes))
    ai = jnp.broadcast_to(a_ref[1:2, :], (SUBLANES, lanes))

    def advance(sr, si, i):
        rows = pl.ds(i, S5_SEGS, stride=S5_SEG_PITCH)
        xr = jnp.concatenate([vbuf[s, rows, :] for s in range(pairs)], axis=1)
        xi = jnp.concatenate([vbuf[pairs + s, rows, :] for s in range(pairs)], axis=1)
        return ar * sr - ai * si + xr, ar * si + ai * sr + xi

    def end_state(i, state):
        return advance(*state, i)

    zero = jnp.zeros((SUBLANES, lanes), F32)
    er, ei = lax.fori_loop(0, seg_len, end_state, (zero, zero), unroll=8)
    ebuf[:, re] = er
    ebuf[:, im] = ei

    atr = a_ref[2:3, :]
    ati = a_ref[3:4, :]
    cr = jnp.zeros((1, lanes), F32)
    ci = jnp.zeros((1, lanes), F32)
    for seg in range(S5_SEGS):
        sinbuf[seg:seg + 1, re] = cr
        sinbuf[seg:seg + 1, im] = ci
        cr, ci = (atr * cr - ati * ci + ebuf[seg:seg + 1, re],
                  atr * ci + ati * cr + ebuf[seg:seg + 1, im])

    def record(i, state):
        sr, si = state
        rows = pl.ds(i, S5_SEGS, stride=S5_SEG_PITCH)
        for s in range(pairs):
            spbuf[s, rows, :] = sr[:, s * LANES:(s + 1) * LANES]
            spbuf[pairs + s, rows, :] = si[:, s * LANES:(s + 1) * LANES]
        return advance(sr, si, i)

    lax.fori_loop(0, seg_len, record, (sinbuf[:, re], sinbuf[:, im]), unroll=8)

    def natural_rows(slab):
        return jnp.concatenate(
            [spbuf[slab, seg * S5_SEG_PITCH:seg * S5_SEG_PITCH + seg_len, :]
             for seg in range(S5_SEGS)], axis=0)

    for q in range(pairs):
        sp = jnp.concatenate([natural_rows(q), natural_rows(pairs + q)], axis=1)
        spt = sp.astype(BF16).T
        for gl in (2 * q, 2 * q + 1):
            rhs = jnp.concatenate([zt[gl], spt], axis=0)
            ytbuf[gl] = jnp.dot(wyt_ref[gl], rhs, preferred_element_type=F32)

    for j in range(S5_BLOCK):
        yt = jnp.concatenate([ytbuf[gl, j * S5_GROUP:(j + 1) * S5_GROUP, :]
                              for gl in range(S5_CHUNK_GROUPS)], axis=0)
        tokbuf[pl.ds(j, n_blk, stride=S5_BLOCK), :] = yt.astype(BF16).T.astype(F32)
    y_ref[...] = tokbuf[...].astype(BF16)


def _s5_params(lam_re, lam_im, log_dt, b_re, b_im, c_re, c_im, seg_len):
    hp = lax.Precision.HIGHEST
    r = S5_BLOCK
    g_n, p_n, h_n = S5_GROUPS, S5_STATE, S5_GROUP
    lr = jnp.minimum(lam_re.astype(F32), -1e-4)
    li = lam_im.astype(F32)
    dt = jnp.exp(log_dt.astype(F32))[:, None]
    mag = jnp.exp(lr * dt)
    ab_re = mag * jnp.cos(li * dt)
    ab_im = mag * jnp.sin(li * dt)
    den = lr * lr + li * li
    n_re = ab_re - 1.0
    n_im = ab_im
    z_re = (n_re * lr + n_im * li) / den
    z_im = (n_im * lr - n_re * li) / den
    br = b_re.astype(F32)
    bi = b_im.astype(F32)
    bb_re = z_re[..., None] * br - z_im[..., None] * bi
    bb_im = z_re[..., None] * bi + z_im[..., None] * br

    ldt = (lr * dt)[:, None, :]
    wdt = (li * dt)[:, None, :]

    def a_pow(k):
        pm = jnp.exp(ldt * k)
        return pm * jnp.cos(wdt * k), pm * jnp.sin(wdt * k)

    pr, pi = a_pow(jnp.arange(r + 1, dtype=F32)[None, :, None])
    prv = pr[:, r - 1::-1]
    piv = pi[:, r - 1::-1]
    cr = c_re.astype(F32)
    ci = c_im.astype(F32)

    car_re = cr[:, :, None, :] * prv[:, None] - ci[:, :, None, :] * piv[:, None]
    car_im = cr[:, :, None, :] * piv[:, None] + ci[:, :, None, :] * prv[:, None]
    kt_rev = (jnp.einsum('gamp,gph->gamh', car_re, bb_re, precision=hp)
              - jnp.einsum('gamp,gph->gamh', car_im, bb_im, precision=hp))
    strip = jnp.pad(kt_rev.reshape(g_n, h_n, r * h_n), ((0, 0), (0, 0), (0, (r - 1) * h_n)))
    mzt = jnp.stack([strip[:, :, (r - 1 - j) * h_n:(r - 1 - j) * h_n + r * h_n] for j in range(r)],
                    axis=1).reshape(g_n, r * h_n, r * h_n)

    cre = (cr[:, None] * pr[:, 1:, None, :] - ci[:, None] * pi[:, 1:, None, :]).reshape(
        g_n, r * h_n, p_n)
    cim = -(cr[:, None] * pi[:, 1:, None, :] + ci[:, None] * pr[:, 1:, None, :]).reshape(
        g_n, r * h_n, p_n)
    zc = jnp.zeros_like(cre)
    even = jnp.concatenate([cre, zc, cim, zc], axis=2)
    odd = jnp.concatenate([zc, cre, zc, cim], axis=2)
    is_odd = (jnp.arange(g_n) % 2 == 1)[:, None, None]
    wyt = jnp.concatenate([mzt, jnp.where(is_odd, odd, even)], axis=2).astype(BF16)

    bbr = jnp.transpose(bb_re, (0, 2, 1))[:, None]
    bbi = jnp.transpose(bb_im, (0, 2, 1))[:, None]
    prr = prv[:, :, None, :]
    pir = piv[:, :, None, :]
    bp_re = (prr * bbr - pir * bbi).reshape(g_n // 2, 2, r * h_n, p_n)
    bp_im = (prr * bbi + pir * bbr).reshape(g_n // 2, 2, r * h_n, p_n)
    zb = jnp.zeros_like(bp_re[:, 0])
    top = jnp.concatenate([bp_re[:, 0], zb, bp_im[:, 0], zb], axis=2)
    bot = jnp.concatenate([zb, bp_re[:, 1], zb, bp_im[:, 1]], axis=2)
    bp = jnp.concatenate([top, bot], axis=1).astype(BF16)

    at_re, at_im = a_pow(float(r * seg_len))
    n_chunks = g_n // S5_CHUNK_GROUPS
    a_tab = jnp.stack([x.reshape(n_chunks, S5_CHUNK_LANES)
                       for x in (pr[:, r], pi[:, r], at_re, at_im)], axis=1)
    return wyt, bp, a_tab


def _s5_branch(u, wyt, bp, a_tab, bsz, seq):
    n_blk = seq // S5_BLOCK
    seg_len = n_blk // S5_SEGS
    width = S5_BLOCK * S5_GROUP
    slab = S5_CHUNK_GROUPS * S5_GROUP
    scan_rows = S5_SEGS * S5_SEG_PITCH
    assert slab == LANES and S5_GROUPS // S5_CHUNK_GROUPS == S5_SLABS
    tok = pl.BlockSpec((None, seq, slab), lambda c, b: (c, b, 0))
    return pl.pallas_call(
        functools.partial(_s5_kernel, n_blk=n_blk, seg_len=seg_len),
        grid=(S5_GROUPS // S5_CHUNK_GROUPS, bsz),
        in_specs=[tok,
                  pl.BlockSpec((S5_CHUNK_GROUPS, width, 2 * width), lambda c, b: (c, 0, 0)),
                  pl.BlockSpec((S5_CHUNK_PAIRS, 2 * width, width), lambda c, b: (c, 0, 0)),
                  pl.BlockSpec((None, 4, S5_CHUNK_LANES), lambda c, b: (c, 0, 0))],
        out_specs=tok,
        out_shape=jax.ShapeDtypeStruct(u.shape, BF16),
        scratch_shapes=[
            pltpu.VMEM((seq, slab), F32),
            pltpu.VMEM((S5_CHUNK_GROUPS, width, n_blk), BF16),
            pltpu.VMEM((S5_CHUNK_GROUPS, width, n_blk), F32),
            pltpu.VMEM((2 * S5_CHUNK_PAIRS, scan_rows, LANES), F32),
            pltpu.VMEM((2 * S5_CHUNK_PAIRS, scan_rows, LANES), F32),
            pltpu.VMEM((S5_SEGS, 2 * S5_CHUNK_LANES), F32),
            pltpu.VMEM((S5_SEGS, 2 * S5_CHUNK_LANES), F32),
        ],
        compiler_params=pltpu.CompilerParams(
            dimension_semantics=("arbitrary", "arbitrary"), vmem_limit_bytes=VMEM_LIMIT_BYTES),
        name="s5_branch",
    )(u, wyt, bp, a_tab)


def _retention_kernel(q_ref, k_ref, kz_ref, v_ref, bz_ref, gain_ref, decay_ref, xi_ref,
                      yb_ref, state, *, chunk_decay):
    c = pl.program_id(1)

    @pl.when(c == 0)
    def _():
        state[...] = jnp.zeros_like(state)

    for ck in range(RET_STEP_CHUNKS):
        rows = slice(ck * RET_CHUNK, (ck + 1) * RET_CHUNK)
        for h in range(RET_HEADS):
            cols = slice(h * RET_DK, (h + 1) * RET_DK)
            qh = q_ref[rows, cols]
            vh = v_ref[rows, cols]
            scores = lax.dot_general(qh, k_ref[rows, cols], (((1,), (1,)), ((), ())),
                                     preferred_element_type=F32)
            inner = jnp.dot((scores * decay_ref[h]).astype(BF16), vh, preferred_element_type=F32)
            prev = state[h]
            cross = jnp.dot(qh, prev.astype(BF16), preferred_element_type=F32) * xi_ref[:, cols]
            local = lax.dot_general(kz_ref[rows, cols], vh, (((0,), (0,)), ((), ())),
                                    preferred_element_type=F32)
            state[h] = prev * chunk_decay[h] + local
            o = inner + cross
            mu = jnp.mean(o, axis=-1, keepdims=True)
            oc = o - mu
            var = jnp.mean(oc * oc, axis=-1, keepdims=True)
            o = oc * lax.rsqrt(var + NORM_EPS) * gain_ref[:, cols]
            yb_ref[rows, cols] = (o * bz_ref[rows, cols].astype(F32)).astype(BF16)


def _retention_tables():
    log_g = np.log1p(-np.exp2(-5.0 - np.arange(RET_HEADS, dtype=np.float64)))
    idx = np.arange(RET_CHUNK, dtype=np.float64)
    diff = idx[:, None] - idx[None, :]
    decay = np.where(diff >= 0, np.exp(log_g[:, None, None] * np.maximum(diff, 0.0)), 0.0)
    xi = np.repeat(np.exp(log_g[None, :] * (idx[:, None] + 1.0)), RET_DK, axis=1)
    zeta = np.repeat(np.exp(log_g[None, :] * (RET_CHUNK - 1.0 - idx[:, None])), RET_DK, axis=1)
    zeta = np.tile(zeta, (PROJ_TILE // RET_CHUNK, 1))
    chunk_decay = tuple(float(x) for x in np.exp(log_g * RET_CHUNK))
    return (jnp.asarray(decay, F32), jnp.asarray(xi, F32), jnp.asarray(zeta, F32), chunk_decay)


def _retention_branch(q, k, kz, v, bz, gain, decay, xi, chunk_decay, bsz, seq):
    step = RET_STEP_CHUNKS * RET_CHUNK
    steps = seq // step
    tok = pl.BlockSpec((step, D_MODEL), lambda b, c: (b * steps + c, 0))
    return pl.pallas_call(
        functools.partial(_retention_kernel, chunk_decay=chunk_decay),
        grid=(bsz, steps),
        in_specs=[tok, tok, tok, tok, tok, _const_spec((1, D_MODEL), False),
                  _const_spec(decay.shape, False), _const_spec(xi.shape, False)],
        out_specs=tok,
        out_shape=jax.ShapeDtypeStruct(q.shape, BF16),
        scratch_shapes=[pltpu.VMEM((RET_HEADS, RET_DK, RET_DV), F32)],
        compiler_params=pltpu.CompilerParams(
            dimension_semantics=("arbitrary", "arbitrary"), vmem_limit_bytes=VMEM_LIMIT_BYTES),
        name="retention_branch",
    )(q, k, kz, v, bz, gain, decay, xi)


def _even_out_kernel(x_ref, ys_ref, u_ref, az_ref, yb_ref, d_ref, wglu_ref, bglu_ref, w_ref,
                     o_ref):
    ys = jnp.concatenate([ys_ref[s] for s in range(S5_SLABS)], axis=1)
    u = jnp.concatenate([u_ref[s] for s in range(S5_SLABS)], axis=1)
    y = ys.astype(F32) + d_ref[...] * u.astype(F32)
    y = jax.nn.gelu(y)
    gl = jnp.dot(y.astype(BF16), wglu_ref[...], preferred_element_type=F32) + bglu_ref[...]
    ya = (y * jax.nn.sigmoid(gl) * az_ref[...].astype(F32)).astype(BF16)
    acc = jnp.dot(ya, w_ref[0:S5_WIDTH, :], preferred_element_type=F32)
    acc = acc + jnp.dot(yb_ref[...], w_ref[S5_WIDTH:, :], preferred_element_type=F32)
    o_ref[...] = x_ref[...] + acc


def _even_out_proj(x2d, ys, u, az, yb, d_skip, w_glu, b_glu, w_out):
    n = x2d.shape[0]
    tm = OUT_TILE
    tok = pl.BlockSpec((tm, D_MODEL), lambda i: (i, 0))
    slabs = pl.BlockSpec((S5_SLABS, tm, LANES), lambda i: (0, i, 0))
    return pl.pallas_call(
        _even_out_kernel,
        grid=(n // tm,),
        in_specs=[tok, slabs, slabs, tok, tok, _const_spec((1, S5_WIDTH), False),
                  _const_spec(w_glu.shape, False), _const_spec((1, S5_WIDTH), False),
                  _const_spec(w_out.shape, False)],
        out_specs=tok,
        out_shape=jax.ShapeDtypeStruct(x2d.shape, F32),
        compiler_params=pltpu.CompilerParams(
            dimension_semantics=("arbitrary",), vmem_limit_bytes=VMEM_LIMIT_BYTES),
        name="even_out_proj",
    )(x2d, ys, u, az, yb, d_skip, w_glu, b_glu, w_out)


def _odd_kernel(x_ref, g_ref, w_ref, vgain_ref, wsp_ref, bsp_ref, wout_ref, fg_ref, o_ref,
                vbuf, ubuf, gbuf, vnbuf, ybuf):
    x = x_ref[...]
    hn = _rms_norm(x, g_ref[...]).astype(BF16)
    gd = SGU_GROUP_DIM

    def proj(col):
        return jnp.dot(hn, w_ref[:, col:col + gd], preferred_element_type=F32)

    for g in range(SGU_GROUPS):
        vbuf[:, g * gd:(g + 1) * gd] = jax.nn.gelu(proj(SGU_WIDTH + g * gd))
    for g in range(SGU_GROUPS):
        ubuf[:, g * gd:(g + 1) * gd] = jax.nn.gelu(proj(g * gd))
        gbuf[:, g * gd:(g + 1) * gd] = _silu(proj(2 * SGU_WIDTH + g * gd))
    v = vbuf[...]
    mu = jnp.mean(v, axis=-1, keepdims=True)
    vc = v - mu
    var = jnp.mean(vc * vc, axis=-1, keepdims=True)
    vnbuf[...] = (vc * lax.rsqrt(var + NORM_EPS) * vgain_ref[...]).astype(BF16)

    row = lax.broadcasted_iota(jnp.int32, (SGU_CHUNK, SGU_CHUNK), 0)
    col = lax.broadcasted_iota(jnp.int32, (SGU_CHUNK, SGU_CHUNK), 1)
    for g in range(SGU_GROUPS):
        cols = slice(g * gd, (g + 1) * gd)
        wm = jnp.where(row >= col, wsp_ref[g], 0.0).astype(BF16)
        bias = bsp_ref[g]
        for c in range(ODD_TILE // SGU_CHUNK):
            rows = slice(c * SGU_CHUNK, (c + 1) * SGU_CHUNK)
            s = jnp.dot(wm, vnbuf[rows, cols], preferred_element_type=F32) + bias
            ybuf[rows, cols] = (ubuf[rows, cols] * s * gbuf[rows, cols]).astype(BF16)

    xn = x + jnp.dot(ybuf[...], wout_ref[...], preferred_element_type=F32)
    o_ref[...] = _rms_norm(xn, fg_ref[...])


def _odd_layer(x2d, gain, w_in, vgain, wsp, bsp, w_out, final_gain):
    n = x2d.shape[0]
    tm = ODD_TILE
    tok = pl.BlockSpec((tm, D_MODEL), lambda i: (i, 0))
    return pl.pallas_call(
        _odd_kernel,
        grid=(n // tm,),
        in_specs=[tok, _const_spec((1, D_MODEL)), _const_spec(w_in.shape),
                  _const_spec((1, SGU_WIDTH)), _const_spec(wsp.shape), _const_spec(bsp.shape),
                  _const_spec(w_out.shape), _const_spec((1, D_MODEL))],
        out_specs=tok,
        out_shape=jax.ShapeDtypeStruct(x2d.shape, F32),
        scratch_shapes=[pltpu.VMEM((tm, SGU_WIDTH), F32),
                        pltpu.VMEM((tm, SGU_WIDTH), F32),
                        pltpu.VMEM((tm, SGU_WIDTH), F32),
                        pltpu.VMEM((tm, SGU_WIDTH), BF16),
                        pltpu.VMEM((tm, SGU_WIDTH), BF16)],
        compiler_params=pltpu.CompilerParams(
            dimension_semantics=("arbitrary",), vmem_limit_bytes=VMEM_LIMIT_BYTES),
        name="odd_layer",
    )(x2d, gain, w_in, vgain, wsp, bsp, w_out, final_gain)


def _rotary_tables(seq):
    half = RET_DK // 2
    pos = np.arange(seq, dtype=np.float64)
    inv = ROPE_BASE ** (-np.arange(half, dtype=np.float64) / half)
    ang = pos[:, None] * inv[None, :]
    return jnp.asarray(np.cos(ang), F32), jnp.asarray(np.sin(ang), F32)


def kernel(x, norm_even, w_in_even, s5_lam_re, s5_lam_im, s5_log_dt, s5_b_re, s5_b_im, s5_c_re, s5_c_im, s5_d, s5_w_glu, s5_b_glu, ret_gn_gain, w_out_even, norm_odd, w_in_odd, sgu_norm_gain, sgu_w_spatial, sgu_b_spatial, w_out_odd, final_norm):
    bsz, seq, d = x.shape
    x2d = x.reshape(bsz * seq, d)
    cos, sin = _rotary_tables(seq)

    decay, xi, zeta, chunk_decay = _retention_tables()
    u, az, q, k, kz, v, bz = _even_in_proj(
        x2d, norm_even[0].reshape(1, d), w_in_even[0].astype(BF16), cos, sin, zeta, seq)
    seg_len = seq // S5_BLOCK // S5_SEGS
    wyt, bp, a_tab = _s5_params(s5_lam_re[0], s5_lam_im[0], s5_log_dt[0], s5_b_re[0], s5_b_im[0],
                                s5_c_re[0], s5_c_im[0], seg_len)
    ys = _s5_branch(u, wyt, bp, a_tab, bsz, seq)
    yb = _retention_branch(q, k, kz, v, bz, ret_gn_gain[0].reshape(1, -1).astype(F32),
                           decay, xi, chunk_decay, bsz, seq)
    x2d = _even_out_proj(x2d, ys, u, az, yb, s5_d[0].reshape(1, -1).astype(F32),
                         s5_w_glu[0].astype(BF16), s5_b_glu[0].reshape(1, -1).astype(F32),
                         w_out_even[0].astype(BF16))

    out = _odd_layer(x2d, norm_odd[0].reshape(1, d), w_in_odd[0].astype(BF16),
                     sgu_norm_gain[0].reshape(1, -1).astype(F32), sgu_w_spatial[0].astype(F32),
                     sgu_b_spatial[0].astype(F32)[:, :, None], w_out_odd[0].astype(BF16),
                     final_norm.reshape(1, d))
    return out.reshape(bsz, seq, d)
```

```python
import functools

import jax
import jax.numpy as jnp
import numpy as np
from jax import lax
from jax.experimental import pallas as pl
from jax.experimental.pallas import tpu as pltpu

F32 = jnp.float32
BF16 = jnp.bfloat16

D_MODEL = 1024
S5_WIDTH = 1024
S5_GROUP = 16
S5_GROUPS = 64
S5_STATE = 64
RET_HEADS = 4
RET_DK = 256
RET_DV = 256
ROPE_BASE = 10000.0
SGU_WIDTH = 2048
SGU_GROUPS = 4
SGU_GROUP_DIM = 512
SGU_CHUNK = 128
NORM_EPS = 1e-6

V7X_VMEM_BYTES = 64 * 1024 * 1024
VMEM_LIMIT_BYTES = V7X_VMEM_BYTES - 12 * 1024 * 1024
SUBLANES = 8
LANES = 128

PROJ_TILE = 512
OUT_TILE = 512
S5_BLOCK = 16
S5_SEGS = SUBLANES
S5_CHUNK_GROUPS = 8
S5_CHUNK_PAIRS = S5_CHUNK_GROUPS // 2
S5_CHUNK_LANES = S5_CHUNK_GROUPS * S5_STATE
S5_SLABS = S5_WIDTH // LANES
S5_SEG_PITCH = 72
RET_CHUNK = 256
RET_STEP_CHUNKS = 4
ODD_TILE = 512


def _const_spec(shape, single_buffer=True):
    zeros = (0,) * len(shape)
    if single_buffer:
        return pl.BlockSpec(shape, lambda *_: zeros, pipeline_mode=pl.Buffered(1))
    return pl.BlockSpec(shape, lambda *_: zeros)


def _rms_norm(x, gain):
    ms = jnp.mean(x * x, axis=-1, keepdims=True)
    return x * lax.rsqrt(ms + NORM_EPS) * gain


def _silu(x):
    return x * jax.nn.sigmoid(x)


def _even_in_kernel(x_ref, g_ref, w_ref, cos_ref, sin_ref, zeta_ref,
                    u_ref, az_ref, q_ref, k_ref, kz_ref, v_ref, bz_ref):
    hn = _rms_norm(x_ref[...], g_ref[...]).astype(BF16)

    def proj(j):
        return jnp.dot(hn, w_ref[:, j * D_MODEL:(j + 1) * D_MODEL], preferred_element_type=F32)

    def rotary_store(p, out_ref, scale, decayed_ref=None):
        cos = cos_ref[...]
        sin = sin_ref[...]
        half = RET_DK // 2
        for h in range(RET_HEADS):
            lo = slice(h * RET_DK, h * RET_DK + half)
            hi = slice(h * RET_DK + half, (h + 1) * RET_DK)
            x1 = p[:, lo]
            x2 = p[:, hi]
            o1 = x1 * cos - x2 * sin
            o2 = x1 * sin + x2 * cos
            if scale != 1.0:
                o1 = o1 * scale
                o2 = o2 * scale
            out_ref[:, lo] = o1.astype(BF16)
            out_ref[:, hi] = o2.astype(BF16)
            if decayed_ref is not None:
                decayed_ref[:, lo] = (o1 * zeta_ref[:, lo]).astype(BF16)
                decayed_ref[:, hi] = (o2 * zeta_ref[:, hi]).astype(BF16)

    u = proj(0).astype(BF16)
    for s in range(S5_SLABS):
        u_ref[s] = u[:, s * LANES:(s + 1) * LANES]
    az_ref[...] = _silu(proj(1)).astype(BF16)
    rotary_store(proj(2), q_ref, 1.0)
    rotary_store(proj(3), k_ref, RET_DK ** -0.5, kz_ref)
    v_ref[...] = proj(4).astype(BF16)
    bz_ref[...] = _silu(proj(5)).astype(BF16)


def _even_in_proj(x2d, gain, w_in, cos, sin, zeta, seq):
    n = x2d.shape[0]
    tm = PROJ_TILE
    tiles_per_seq = seq // tm
    tok = pl.BlockSpec((tm, D_MODEL), lambda i: (i, 0))
    rot = pl.BlockSpec((tm, RET_DK // 2), lambda i: (i % tiles_per_seq, 0))
    out = jax.ShapeDtypeStruct((n, D_MODEL), BF16)
    return pl.pallas_call(
        _even_in_kernel,
        grid=(n // tm,),
        in_specs=[tok, _const_spec((1, D_MODEL)), _const_spec(w_in.shape), rot, rot,
                  _const_spec(zeta.shape)],
        out_specs=[pl.BlockSpec((S5_SLABS, tm, LANES), lambda i: (0, i, 0))] + [tok] * 6,
        out_shape=[jax.ShapeDtypeStruct((S5_SLABS, n, LANES), BF16)] + [out] * 6,
        compiler_params=pltpu.CompilerParams(
            dimension_semantics=("arbitrary",), vmem_limit_bytes=VMEM_LIMIT_BYTES),
        name="even_in_proj",
    )(x2d, gain, w_in, cos, sin, zeta)


def _s5_kernel(u_ref, kw_ref, cp_ref, bpg_ref, pcon_ref, econ_ref, a_ref, y_ref,
               tokbuf, zt, ytbuf, vbuf, spbuf, ebuf, sinbuf, wyt, bp, *, n_blk, seg_len):
    lanes = S5_CHUNK_LANES
    re = slice(0, lanes)
    im = slice(lanes, 2 * lanes)
    pairs = S5_CHUNK_PAIRS
    blk_w = S5_BLOCK * S5_GROUP

    @pl.when(pl.program_id(1) == 0)
    def _():
        kw = kw_ref[...].reshape(S5_CHUNK_GROUPS * S5_GROUP, blk_w)
        for jo in range(S5_BLOCK):
            rows = jnp.dot(kw, pcon_ref[jo], preferred_element_type=F32).astype(BF16)
            for gl in range(S5_CHUNK_GROUPS):
                wyt[gl, jo * S5_GROUP:(jo + 1) * S5_GROUP, 0:blk_w] = (
                    rows[gl * S5_GROUP:(gl + 1) * S5_GROUP, :])
        for gl in range(S5_CHUNK_GROUPS):
            wyt[gl, :, blk_w:] = cp_ref[gl]
        for q in range(pairs):
            for par in range(2):
                bp[q, par * blk_w:(par + 1) * blk_w, :] = jnp.dot(
                    bpg_ref[2 * q + par], econ_ref[par], preferred_element_type=F32).astype(BF16)

    tokbuf[...] = u_ref[...].astype(F32)
    for j in range(S5_BLOCK):
        ujt = tokbuf[pl.ds(j, n_blk, stride=S5_BLOCK), :].astype(BF16).T
        for gl in range(S5_CHUNK_GROUPS):
            zt[gl, j * S5_GROUP:(j + 1) * S5_GROUP, :] = ujt[gl * S5_GROUP:(gl + 1) * S5_GROUP, :]

    for q in range(pairs):
        zc = jnp.concatenate([zt[2 * q], zt[2 * q + 1]], axis=0)
        v = lax.dot_general(zc, bp[q], (((0,), (0,)), ((), ())),
                            preferred_element_type=F32)
        for seg in range(S5_SEGS):
            src = slice(seg * seg_len, (seg + 1) * seg_len)
            dst = slice(seg * S5_SEG_PITCH, seg * S5_SEG_PITCH + seg_len)
            vbuf[q, dst, :] = v[src, 0:LANES]
            vbuf[pairs + q, dst, :] = v[src, LANES:]

    ar = jnp.broadcast_to(a_ref[0:1, :], (SUBLANES, lanes))
    ai = jnp.broadcast_to(a_ref[1:2, :], (SUBLANES, lanes))

    def advance(sr, si, i):
        rows = pl.ds(i, S5_SEGS, stride=S5_SEG_PITCH)
        xr = jnp.concatenate([vbuf[s, rows, :] for s in range(pairs)], axis=1)
        xi = jnp.concatenate([vbuf[pairs + s, rows, :] for s in range(pairs)], axis=1)
        return ar * sr - ai * si + xr, ar * si + ai * sr + xi

    def end_state(i, state):
        return advance(*state, i)

    zero = jnp.zeros((SUBLANES, lanes), F32)
    er, ei = lax.fori_loop(0, seg_len, end_state, (zero, zero), unroll=8)
    ebuf[:, re] = er
    ebuf[:, im] = ei

    atr = a_ref[2:3, :]
    ati = a_ref[3:4, :]
    cr = jnp.zeros((1, lanes), F32)
    ci = jnp.zeros((1, lanes), F32)
    for seg in range(S5_SEGS):
        sinbuf[seg:seg + 1, re] = cr
        sinbuf[seg:seg + 1, im] = ci
        cr, ci = (atr * cr - ati * ci + ebuf[seg:seg + 1, re],
                  atr * ci + ati * cr + ebuf[seg:seg + 1, im])

    def record(i, state):
        sr, si = state
        rows = pl.ds(i, S5_SEGS, stride=S5_SEG_PITCH)
        for s in range(pairs):
            spbuf[s, rows, :] = sr[:, s * LANES:(s + 1) * LANES]
            spbuf[pairs + s, rows, :] = si[:, s * LANES:(s + 1) * LANES]
        return advance(sr, si, i)

    lax.fori_loop(0, seg_len, record, (sinbuf[:, re], sinbuf[:, im]), unroll=8)

    def natural_rows(slab):
        return jnp.concatenate(
            [spbuf[slab, seg * S5_SEG_PITCH:seg * S5_SEG_PITCH + seg_len, :]
             for seg in range(S5_SEGS)], axis=0)

    for q in range(pairs):
        sp = jnp.concatenate([natural_rows(q), natural_rows(pairs + q)], axis=1)
        spt = sp.astype(BF16).T
        for par in range(2):
            gl = 2 * q + par
            rhs = jnp.concatenate(
                [zt[gl], spt[par * S5_STATE:(par + 1) * S5_STATE],
                 spt[LANES + par * S5_STATE:LANES + (par + 1) * S5_STATE]], axis=0)
            ytbuf[gl] = jnp.dot(wyt[gl], rhs, preferred_element_type=F32)

    for j in range(S5_BLOCK):
        yt = jnp.concatenate([ytbuf[gl, j * S5_GROUP:(j + 1) * S5_GROUP, :]
                              for gl in range(S5_CHUNK_GROUPS)], axis=0)
        tokbuf[pl.ds(j, n_blk, stride=S5_BLOCK), :] = yt.astype(BF16).T.astype(F32)
    y_ref[...] = tokbuf[...].astype(BF16)


def _s5_params(lam_re, lam_im, log_dt, b_re, b_im, c_re, c_im, seg_len):
    hp = lax.Precision.HIGHEST
    r = S5_BLOCK
    g_n, p_n, h_n = S5_GROUPS, S5_STATE, S5_GROUP
    lr = jnp.minimum(lam_re.astype(F32), -1e-4)
    li = lam_im.astype(F32)
    dt = jnp.exp(log_dt.astype(F32))[:, None]
    mag = jnp.exp(lr * dt)
    ab_re = mag * jnp.cos(li * dt)
    ab_im = mag * jnp.sin(li * dt)
    den = lr * lr + li * li
    n_re = ab_re - 1.0
    n_im = ab_im
    z_re = (n_re * lr + n_im * li) / den
    z_im = (n_im * lr - n_re * li) / den
    br = b_re.astype(F32)
    bi = b_im.astype(F32)
    bb_re = z_re[..., None] * br - z_im[..., None] * bi
    bb_im = z_re[..., None] * bi + z_im[..., None] * br

    ldt = (lr * dt)[:, None, :]
    wdt = (li * dt)[:, None, :]

    def a_pow(k):
        pm = jnp.exp(ldt * k)
        return pm * jnp.cos(wdt * k), pm * jnp.sin(wdt * k)

    pr, pi = a_pow(jnp.arange(r + 1, dtype=F32)[None, :, None])
    cr = c_re.astype(F32)
    ci = c_im.astype(F32)

    car_re = cr[:, :, None, :] * pr[:, None, :r] - ci[:, :, None, :] * pi[:, None, :r]
    car_im = cr[:, :, None, :] * pi[:, None, :r] + ci[:, :, None, :] * pr[:, None, :r]
    kt = (jnp.einsum('galp,gph->galh', car_re, bb_re, precision=hp)
          - jnp.einsum('galp,gph->galh', car_im, bb_im, precision=hp))
    kw = kt.reshape(g_n, h_n, r * h_n).astype(BF16)

    cre = (cr[:, None] * pr[:, 1:, None, :] - ci[:, None] * pi[:, 1:, None, :]).reshape(
        g_n, r * h_n, p_n)
    cim = -(cr[:, None] * pi[:, 1:, None, :] + ci[:, None] * pr[:, 1:, None, :]).reshape(
        g_n, r * h_n, p_n)
    cp = jnp.concatenate([cre, cim], axis=2).astype(BF16)

    bbr = jnp.transpose(bb_re, (0, 2, 1))[:, None]
    bbi = jnp.transpose(bb_im, (0, 2, 1))[:, None]
    prr = pr[:, r - 1::-1][:, :, None, :]
    pir = pi[:, r - 1::-1][:, :, None, :]
    bpg = jnp.concatenate([(prr * bbr - pir * bbi).reshape(g_n, r * h_n, p_n),
                           (prr * bbi + pir * bbr).reshape(g_n, r * h_n, p_n)],
                          axis=2).astype(BF16)

    at_re, at_im = a_pow(float(r * seg_len))
    n_chunks = g_n // S5_CHUNK_GROUPS
    a_tab = jnp.stack([x.reshape(n_chunks, S5_CHUNK_LANES)
                       for x in (pr[:, r], pi[:, r], at_re, at_im)], axis=1)
    return kw, cp, bpg, a_tab


def _s5_placement_tables():
    r, h_n, p_n = S5_BLOCK, S5_GROUP, S5_STATE
    pcon = np.zeros((r, r, h_n, r, h_n), np.float32)
    for jo in range(r):
        for ji in range(jo + 1):
            pcon[jo, jo - ji, np.arange(h_n), ji, np.arange(h_n)] = 1.0
    econ = np.zeros((2, 2 * p_n, 4 * p_n), np.float32)
    for par in range(2):
        econ[par, np.arange(p_n), par * p_n + np.arange(p_n)] = 1.0
        econ[par, p_n + np.arange(p_n), (2 + par) * p_n + np.arange(p_n)] = 1.0
    return (jnp.asarray(pcon.reshape(r, r * h_n, r * h_n), BF16), jnp.asarray(econ, BF16))


def _s5_branch(u, kw, cp, bpg, a_tab, bsz, seq):
    n_blk = seq // S5_BLOCK
    seg_len = n_blk // S5_SEGS
    width = S5_BLOCK * S5_GROUP
    slab = S5_CHUNK_GROUPS * S5_GROUP
    scan_rows = S5_SEGS * S5_SEG_PITCH
    assert slab == LANES and S5_GROUPS // S5_CHUNK_GROUPS == S5_SLABS
    pcon, econ = _s5_placement_tables()
    tok = pl.BlockSpec((None, seq, slab), lambda c, b: (c, b, 0))

    def per_chunk(shape):
        return pl.BlockSpec((S5_CHUNK_GROUPS,) + shape, lambda c, b: (c, 0, 0))

    return pl.pallas_call(
        functools.partial(_s5_kernel, n_blk=n_blk, seg_len=seg_len),
        grid=(S5_GROUPS // S5_CHUNK_GROUPS, bsz),
        in_specs=[tok, per_chunk((S5_GROUP, width)), per_chunk((width, 2 * S5_STATE)),
                  per_chunk((width, 2 * S5_STATE)), _const_spec(pcon.shape), _const_spec(econ.shape),
                  pl.BlockSpec((None, 4, S5_CHUNK_LANES), lambda c, b: (c, 0, 0))],
        out_specs=tok,
        out_shape=jax.ShapeDtypeStruct(u.shape, BF16),
        scratch_shapes=[
            pltpu.VMEM((seq, slab), F32),
            pltpu.VMEM((S5_CHUNK_GROUPS, width, n_blk), BF16),
            pltpu.VMEM((S5_CHUNK_GROUPS, width, n_blk), F32),
            pltpu.VMEM((2 * S5_CHUNK_PAIRS, scan_rows, LANES), F32),
            pltpu.VMEM((2 * S5_CHUNK_PAIRS, scan_rows, LANES), F32),
            pltpu.VMEM((S5_SEGS, 2 * S5_CHUNK_LANES), F32),
            pltpu.VMEM((S5_SEGS, 2 * S5_CHUNK_LANES), F32),
            pltpu.VMEM((S5_CHUNK_GROUPS, width, width + 2 * S5_STATE), BF16),
            pltpu.VMEM((S5_CHUNK_PAIRS, 2 * width, 4 * S5_STATE), BF16),
        ],
        compiler_params=pltpu.CompilerParams(
            dimension_semantics=("arbitrary", "arbitrary"), vmem_limit_bytes=VMEM_LIMIT_BYTES),
        name="s5_branch",
    )(u, kw, cp, bpg, pcon, econ, a_tab)


def _retention_kernel(q_ref, k_ref, kz_ref, v_ref, bz_ref, gain_ref, decay_ref, xi_ref,
                      yb_ref, state, *, chunk_decay):
    c = pl.program_id(1)

    @pl.when(c == 0)
    def _():
        state[...] = jnp.zeros_like(state)

    for ck in range(RET_STEP_CHUNKS):
        rows = slice(ck * RET_CHUNK, (ck + 1) * RET_CHUNK)
        for h in range(RET_HEADS):
            cols = slice(h * RET_DK, (h + 1) * RET_DK)
            qh = q_ref[rows, cols]
            vh = v_ref[rows, cols]
            scores = lax.dot_general(qh, k_ref[rows, cols], (((1,), (1,)), ((), ())),
                                     preferred_element_type=F32)
            inner = jnp.dot((scores * decay_ref[h]).astype(BF16), vh, preferred_element_type=F32)
            prev = state[h]
            cross = jnp.dot(qh, prev.astype(BF16), preferred_element_type=F32) * xi_ref[:, cols]
            local = lax.dot_general(kz_ref[rows, cols], vh, (((0,), (0,)), ((), ())),
                                    preferred_element_type=F32)
            state[h] = prev * chunk_decay[h] + local
            o = inner + cross
            mu = jnp.mean(o, axis=-1, keepdims=True)
            oc = o - mu
            var = jnp.mean(oc * oc, axis=-1, keepdims=True)
            o = oc * lax.rsqrt(var + NORM_EPS) * gain_ref[:, cols]
            yb_ref[rows, cols] = (o * bz_ref[rows, cols].astype(F32)).astype(BF16)


def _retention_tables():
    log_g = np.log1p(-np.exp2(-5.0 - np.arange(RET_HEADS, dtype=np.float64)))
    idx = np.arange(RET_CHUNK, dtype=np.float64)
    diff = idx[:, None] - idx[None, :]
    decay = np.where(diff >= 0, np.exp(log_g[:, None, None] * np.maximum(diff, 0.0)), 0.0)
    xi = np.repeat(np.exp(log_g[None, :] * (idx[:, None] + 1.0)), RET_DK, axis=1)
    zeta = np.repeat(np.exp(log_g[None, :] * (RET_CHUNK - 1.0 - idx[:, None])), RET_DK, axis=1)
    zeta = np.tile(zeta, (PROJ_TILE // RET_CHUNK, 1))
    chunk_decay = tuple(float(x) for x in np.exp(log_g * RET_CHUNK))
    return (jnp.asarray(decay, F32), jnp.asarray(xi, F32), jnp.asarray(zeta, F32), chunk_decay)


def _retention_branch(q, k, kz, v, bz, gain, decay, xi, chunk_decay, bsz, seq):
    step = RET_STEP_CHUNKS * RET_CHUNK
    steps = seq // step
    tok = pl.BlockSpec((step, D_MODEL), lambda b, c: (b * steps + c, 0))
    return pl.pallas_call(
        functools.partial(_retention_kernel, chunk_decay=chunk_decay),
        grid=(bsz, steps),
        in_specs=[tok, tok, tok, tok, tok, _const_spec((1, D_MODEL), False),
                  _const_spec(decay.shape, False), _const_spec(xi.shape, False)],
        out_specs=tok,
        out_shape=jax.ShapeDtypeStruct(q.shape, BF16),
        scratch_shapes=[pltpu.VMEM((RET_HEADS, RET_DK, RET_DV), F32)],
        compiler_params=pltpu.CompilerParams(
            dimension_semantics=("arbitrary", "arbitrary"), vmem_limit_bytes=VMEM_LIMIT_BYTES),
        name="retention_branch",
    )(q, k, kz, v, bz, gain, decay, xi)


def _even_out_kernel(x_ref, ys_ref, u_ref, az_ref, yb_ref, d_ref, wglu_ref, bglu_ref, w_ref,
                     o_ref):
    ys = jnp.concatenate([ys_ref[s] for s in range(S5_SLABS)], axis=1)
    u = jnp.concatenate([u_ref[s] for s in range(S5_SLABS)], axis=1)
    y = ys.astype(F32) + d_ref[...] * u.astype(F32)
    y = jax.nn.gelu(y)
    gl = jnp.dot(y.astype(BF16), wglu_ref[...], preferred_element_type=F32) + bglu_ref[...]
    ya = (y * jax.nn.sigmoid(gl) * az_ref[...].astype(F32)).astype(BF16)
    acc = jnp.dot(ya, w_ref[0:S5_WIDTH, :], preferred_element_type=F32)
    acc = acc + jnp.dot(yb_ref[...], w_ref[S5_WIDTH:, :], preferred_element_type=F32)
    o_ref[...] = x_ref[...] + acc


def _even_out_proj(x2d, ys, u, az, yb, d_skip, w_glu, b_glu, w_out):
    n = x2d.shape[0]
    tm = OUT_TILE
    tok = pl.BlockSpec((tm, D_MODEL), lambda i: (i, 0))
    slabs = pl.BlockSpec((S5_SLABS, tm, LANES), lambda i: (0, i, 0))
    return pl.pallas_call(
        _even_out_kernel,
        grid=(n // tm,),
        in_specs=[tok, slabs, slabs, tok, tok, _const_spec((1, S5_WIDTH), False),
                  _const_spec(w_glu.shape, False), _const_spec((1, S5_WIDTH), False),
                  _const_spec(w_out.shape, False)],
        out_specs=tok,
        out_shape=jax.ShapeDtypeStruct(x2d.shape, F32),
        compiler_params=pltpu.CompilerParams(
            dimension_semantics=("arbitrary",), vmem_limit_bytes=VMEM_LIMIT_BYTES),
        name="even_out_proj",
    )(x2d, ys, u, az, yb, d_skip, w_glu, b_glu, w_out)


def _odd_kernel(x_ref, g_ref, w_ref, vgain_ref, wsp_ref, bsp_ref, wout_ref, fg_ref, o_ref,
                vbuf, ubuf, gbuf, vnbuf, ybuf):
    x = x_ref[...]
    hn = _rms_norm(x, g_ref[...]).astype(BF16)
    gd = SGU_GROUP_DIM

    def proj(col):
        return jnp.dot(hn, w_ref[:, col:col + gd], preferred_element_type=F32)

    for g in range(SGU_GROUPS):
        vbuf[:, g * gd:(g + 1) * gd] = jax.nn.gelu(proj(SGU_WIDTH + g * gd))
    for g in range(SGU_GROUPS):
        ubuf[:, g * gd:(g + 1) * gd] = jax.nn.gelu(proj(g * gd))
        gbuf[:, g * gd:(g + 1) * gd] = _silu(proj(2 * SGU_WIDTH + g * gd))
    v = vbuf[...]
    mu = jnp.mean(v, axis=-1, keepdims=True)
    vc = v - mu
    var = jnp.mean(vc * vc, axis=-1, keepdims=True)
    vnbuf[...] = (vc * lax.rsqrt(var + NORM_EPS) * vgain_ref[...]).astype(BF16)

    row = lax.broadcasted_iota(jnp.int32, (SGU_CHUNK, SGU_CHUNK), 0)
    col = lax.broadcasted_iota(jnp.int32, (SGU_CHUNK, SGU_CHUNK), 1)
    for g in range(SGU_GROUPS):
        cols = slice(g * gd, (g + 1) * gd)
        wm = jnp.where(row >= col, wsp_ref[g], 0.0).astype(BF16)
        bias = bsp_ref[g]
        for c in range(ODD_TILE // SGU_CHUNK):
            rows = slice(c * SGU_CHUNK, (c + 1) * SGU_CHUNK)
            s = jnp.dot(wm, vnbuf[rows, cols], preferred_element_type=F32) + bias
            ybuf[rows, cols] = (ubuf[rows, cols] * s * gbuf[rows, cols]).astype(BF16)

    xn = x + jnp.dot(ybuf[...], wout_ref[...], preferred_element_type=F32)
    o_ref[...] = _rms_norm(xn, fg_ref[...])


def _odd_layer(x2d, gain, w_in, vgain, wsp, bsp, w_out, final_gain):
    n = x2d.shape[0]
    tm = ODD_TILE
    tok = pl.BlockSpec((tm, D_MODEL), lambda i: (i, 0))
    return pl.pallas_call(
        _odd_kernel,
        grid=(n // tm,),
        in_specs=[tok, _const_spec((1, D_MODEL)), _const_spec(w_in.shape),
                  _const_spec((1, SGU_WIDTH)), _const_spec(wsp.shape), _const_spec(bsp.shape),
                  _const_spec(w_out.shape), _const_spec((1, D_MODEL))],
        out_specs=tok,
        out_shape=jax.ShapeDtypeStruct(x2d.shape, F32),
        scratch_shapes=[pltpu.VMEM((tm, SGU_WIDTH), F32),
                        pltpu.VMEM((tm, SGU_WIDTH), F32),
                        pltpu.VMEM((tm, SGU_WIDTH), F32),
                        pltpu.VMEM((tm, SGU_WIDTH), BF16),
                        pltpu.VMEM((tm, SGU_WIDTH), BF16)],
        compiler_params=pltpu.CompilerParams(
            dimension_semantics=("arbitrary",), vmem_limit_bytes=VMEM_LIMIT_BYTES),
        name="odd_layer",
    )(x2d, gain, w_in, vgain, wsp, bsp, w_out, final_gain)


def _rotary_tables(seq):
    half = RET_DK // 2
    pos = np.arange(seq, dtype=np.float64)
    inv = ROPE_BASE ** (-np.arange(half, dtype=np.float64) / half)
    ang = pos[:, None] * inv[None, :]
    return jnp.asarray(np.cos(ang), F32), jnp.asarray(np.sin(ang), F32)


def kernel(x, norm_even, w_in_even, s5_lam_re, s5_lam_im, s5_log_dt, s5_b_re, s5_b_im, s5_c_re, s5_c_im, s5_d, s5_w_glu, s5_b_glu, ret_gn_gain, w_out_even, norm_odd, w_in_odd, sgu_norm_gain, sgu_w_spatial, sgu_b_spatial, w_out_odd, final_norm):
    bsz, seq, d = x.shape
    x2d = x.reshape(bsz * seq, d)
    cos, sin = _rotary_tables(seq)

    decay, xi, zeta, chunk_decay = _retention_tables()
    u, az, q, k, kz, v, bz = _even_in_proj(
        x2d, norm_even[0].reshape(1, d), w_in_even[0].astype(BF16), cos, sin, zeta, seq)
    seg_len = seq // S5_BLOCK // S5_SEGS
    kw, cp, bpg, a_tab = _s5_params(s5_lam_re[0], s5_lam_im[0], s5_log_dt[0], s5_b_re[0],
                                    s5_b_im[0], s5_c_re[0], s5_c_im[0], seg_len)
    ys = _s5_branch(u, kw, cp, bpg, a_tab, bsz, seq)
    yb = _retention_branch(q, k, kz, v, bz, ret_gn_gain[0].reshape(1, -1).astype(F32),
                           decay, xi, chunk_decay, bsz, seq)
    x2d = _even_out_proj(x2d, ys, u, az, yb, s5_d[0].reshape(1, -1).astype(F32),
                         s5_w_glu[0].astype(BF16), s5_b_glu[0].reshape(1, -1).astype(F32),
                         w_out_even[0].astype(BF16))

    out = _odd_layer(x2d, norm_odd[0].reshape(1, d), w_in_odd[0].astype(BF16),
                     sgu_norm_gain[0].reshape(1, -1).astype(F32), sgu_w_spatial[0].astype(F32),
                     sgu_b_spatial[0].astype(F32)[:, :, None], w_out_odd[0].astype(BF16),
                     final_norm.reshape(1, d))
    return out.reshape(bsz, seq, d)
```

```python
import functools

import jax
import jax.numpy as jnp
import numpy as np
from jax import lax
from jax.experimental import pallas as pl
from jax.experimental.pallas import tpu as pltpu

F32 = jnp.float32
BF16 = jnp.bfloat16

D_MODEL = 1024
S5_WIDTH = 1024
S5_GROUP = 16
S5_GROUPS = 64
S5_STATE = 64
RET_HEADS = 4
RET_DK = 256
RET_DV = 256
ROPE_BASE = 10000.0
SGU_WIDTH = 2048
SGU_GROUPS = 4
SGU_GROUP_DIM = 512
SGU_CHUNK = 128
NORM_EPS = 1e-6

V7X_VMEM_BYTES = 64 * 1024 * 1024
VMEM_LIMIT_BYTES = V7X_VMEM_BYTES - 12 * 1024 * 1024
SUBLANES = 8
LANES = 128

PROJ_TILE = 512
OUT_TILE = 512
S5_BLOCK = 16
S5_SEGS = SUBLANES
S5_CHUNK_GROUPS = 8
S5_CHUNK_PAIRS = S5_CHUNK_GROUPS // 2
S5_CHUNK_LANES = S5_CHUNK_GROUPS * S5_STATE
S5_SLABS = S5_WIDTH // LANES
S5_SEG_PITCH = 72
S5_TOK_PITCH = 24
RET_CHUNK = 256
RET_STEP_CHUNKS = 4
ODD_TILE = 512


def _const_spec(shape, single_buffer=True):
    zeros = (0,) * len(shape)
    if single_buffer:
        return pl.BlockSpec(shape, lambda *_: zeros, pipeline_mode=pl.Buffered(1))
    return pl.BlockSpec(shape, lambda *_: zeros)


def _rms_norm(x, gain):
    ms = jnp.mean(x * x, axis=-1, keepdims=True)
    return x * lax.rsqrt(ms + NORM_EPS) * gain


def _silu(x):
    return x * jax.nn.sigmoid(x)


def _even_in_kernel(x_ref, g_ref, w_ref, cos_ref, sin_ref, zeta_ref,
                    u_ref, az_ref, q_ref, k_ref, kz_ref, v_ref, bz_ref):
    hn = _rms_norm(x_ref[...], g_ref[...]).astype(BF16)

    def proj(j):
        return jnp.dot(hn, w_ref[:, j * D_MODEL:(j + 1) * D_MODEL], preferred_element_type=F32)

    def rotary_store(p, out_ref, scale, decayed_ref=None):
        cos = cos_ref[...]
        sin = sin_ref[...]
        half = RET_DK // 2
        for h in range(RET_HEADS):
            lo = slice(h * RET_DK, h * RET_DK + half)
            hi = slice(h * RET_DK + half, (h + 1) * RET_DK)
            x1 = p[:, lo]
            x2 = p[:, hi]
            o1 = x1 * cos - x2 * sin
            o2 = x1 * sin + x2 * cos
            if scale != 1.0:
                o1 = o1 * scale
                o2 = o2 * scale
            out_ref[:, lo] = o1.astype(BF16)
            out_ref[:, hi] = o2.astype(BF16)
            if decayed_ref is not None:
                decayed_ref[:, lo] = (o1 * zeta_ref[:, lo]).astype(BF16)
                decayed_ref[:, hi] = (o2 * zeta_ref[:, hi]).astype(BF16)

    u = proj(0).astype(BF16)
    for s in range(S5_SLABS):
        u_ref[s] = u[:, s * LANES:(s + 1) * LANES]
    az_ref[...] = _silu(proj(1)).astype(BF16)
    rotary_store(proj(2), q_ref, 1.0)
    rotary_store(proj(3), k_ref, RET_DK ** -0.5, kz_ref)
    v_ref[...] = proj(4).astype(BF16)
    bz_ref[...] = _silu(proj(5)).astype(BF16)


def _even_in_proj(x2d, gain, w_in, cos, sin, zeta, seq):
    n = x2d.shape[0]
    tm = PROJ_TILE
    tiles_per_seq = seq // tm
    tok = pl.BlockSpec((tm, D_MODEL), lambda i: (i, 0))
    rot = pl.BlockSpec((tm, RET_DK // 2), lambda i: (i % tiles_per_seq, 0))
    out = jax.ShapeDtypeStruct((n, D_MODEL), BF16)
    return pl.pallas_call(
        _even_in_kernel,
        grid=(n // tm,),
        in_specs=[tok, _const_spec((1, D_MODEL)), _const_spec(w_in.shape), rot, rot,
                  _const_spec(zeta.shape)],
        out_specs=[pl.BlockSpec((S5_SLABS, tm, LANES), lambda i: (0, i, 0))] + [tok] * 6,
        out_shape=[jax.ShapeDtypeStruct((S5_SLABS, n, LANES), BF16)] + [out] * 6,
        compiler_params=pltpu.CompilerParams(
            dimension_semantics=("arbitrary",), vmem_limit_bytes=VMEM_LIMIT_BYTES),
        name="even_in_proj",
    )(x2d, gain, w_in, cos, sin, zeta)


def _s5_kernel(u_ref, kw_ref, cp_ref, bpg_ref, pcon_ref, econ_ref, a_ref, y_ref,
               tokbuf, zt, ytbuf, vbuf, spbuf, ebuf, sinbuf, wyt, bp, *, n_blk, seg_len):
    lanes = S5_CHUNK_LANES
    re = slice(0, lanes)
    im = slice(lanes, 2 * lanes)
    pairs = S5_CHUNK_PAIRS
    blk_w = S5_BLOCK * S5_GROUP

    @pl.when(pl.program_id(1) == 0)
    def _():
        kw = kw_ref[...].reshape(S5_CHUNK_GROUPS * S5_GROUP, blk_w)
        for jo in range(S5_BLOCK):
            rows = jnp.dot(kw, pcon_ref[jo], preferred_element_type=F32).astype(BF16)
            for gl in range(S5_CHUNK_GROUPS):
                wyt[gl, jo * S5_GROUP:(jo + 1) * S5_GROUP, 0:blk_w] = (
                    rows[gl * S5_GROUP:(gl + 1) * S5_GROUP, :])
        for gl in range(S5_CHUNK_GROUPS):
            wyt[gl, :, blk_w:] = cp_ref[gl]
        for q in range(pairs):
            for par in range(2):
                bp[q, par * blk_w:(par + 1) * blk_w, :] = jnp.dot(
                    bpg_ref[2 * q + par], econ_ref[par], preferred_element_type=F32).astype(BF16)

    for b in range(n_blk):
        tokbuf[b * S5_TOK_PITCH:b * S5_TOK_PITCH + S5_BLOCK, :] = (
            u_ref[b * S5_BLOCK:(b + 1) * S5_BLOCK, :].astype(F32))
    for j in range(S5_BLOCK):
        ujt = tokbuf[pl.ds(j, n_blk, stride=S5_TOK_PITCH), :].astype(BF16).T
        for gl in range(S5_CHUNK_GROUPS):
            zt[gl, j * S5_GROUP:(j + 1) * S5_GROUP, :] = ujt[gl * S5_GROUP:(gl + 1) * S5_GROUP, :]

    for q in range(pairs):
        zc = jnp.concatenate([zt[2 * q], zt[2 * q + 1]], axis=0)
        v = lax.dot_general(zc, bp[q], (((0,), (0,)), ((), ())),
                            preferred_element_type=F32)
        for seg in range(S5_SEGS):
            src = slice(seg * seg_len, (seg + 1) * seg_len)
            dst = slice(seg * S5_SEG_PITCH, seg * S5_SEG_PITCH + seg_len)
            vbuf[q, dst, :] = v[src, 0:LANES]
            vbuf[pairs + q, dst, :] = v[src, LANES:]

    ar = jnp.broadcast_to(a_ref[0:1, :], (SUBLANES, lanes))
    ai = jnp.broadcast_to(a_ref[1:2, :], (SUBLANES, lanes))

    def advance(sr, si, i):
        rows = pl.ds(i, S5_SEGS, stride=S5_SEG_PITCH)
        xr = jnp.concatenate([vbuf[s, rows, :] for s in range(pairs)], axis=1)
        xi = jnp.concatenate([vbuf[pairs + s, rows, :] for s in range(pairs)], axis=1)
        return ar * sr - ai * si + xr, ar * si + ai * sr + xi

    def end_state(i, state):
        return advance(*state, i)

    zero = jnp.zeros((SUBLANES, lanes), F32)
    er, ei = lax.fori_loop(0, seg_len, end_state, (zero, zero), unroll=8)
    ebuf[:, re] = er
    ebuf[:, im] = ei

    atr = a_ref[2:3, :]
    ati = a_ref[3:4, :]
    cr = jnp.zeros((1, lanes), F32)
    ci = jnp.zeros((1, lanes), F32)
    for seg in range(S5_SEGS):
        sinbuf[seg:seg + 1, re] = cr
        sinbuf[seg:seg + 1, im] = ci
        cr, ci = (atr * cr - ati * ci + ebuf[seg:seg + 1, re],
                  atr * ci + ati * cr + ebuf[seg:seg + 1, im])

    def record(i, state):
        sr, si = state
        rows = pl.ds(i, S5_SEGS, stride=S5_SEG_PITCH)
        for s in range(pairs):
            spbuf[s, rows, :] = sr[:, s * LANES:(s + 1) * LANES]
            spbuf[pairs + s, rows, :] = si[:, s * LANES:(s + 1) * LANES]
        return advance(sr, si, i)

    lax.fori_loop(0, seg_len, record, (sinbuf[:, re], sinbuf[:, im]), unroll=8)

    def natural_rows(slab):
        return jnp.concatenate(
            [spbuf[slab, seg * S5_SEG_PITCH:seg * S5_SEG_PITCH + seg_len, :]
             for seg in range(S5_SEGS)], axis=0)

    for q in range(pairs):
        sp = jnp.concatenate([natural_rows(q), natural_rows(pairs + q)], axis=1)
        spt = sp.astype(BF16).T
        for par in range(2):
            gl = 2 * q + par
            rhs = jnp.concatenate(
                [zt[gl], spt[par * S5_STATE:(par + 1) * S5_STATE],
                 spt[LANES + par * S5_STATE:LANES + (par + 1) * S5_STATE]], axis=0)
            ytbuf[gl] = jnp.dot(wyt[gl], rhs, preferred_element_type=F32)

    for j in range(S5_BLOCK):
        yt = jnp.concatenate([ytbuf[gl, j * S5_GROUP:(j + 1) * S5_GROUP, :]
                              for gl in range(S5_CHUNK_GROUPS)], axis=0)
        tokbuf[pl.ds(j, n_blk, stride=S5_TOK_PITCH), :] = yt.astype(BF16).T.astype(F32)
    for b in range(n_blk):
        y_ref[b * S5_BLOCK:(b + 1) * S5_BLOCK, :] = (
            tokbuf[b * S5_TOK_PITCH:b * S5_TOK_PITCH + S5_BLOCK, :].astype(BF16))


def _s5_params(lam_re, lam_im, log_dt, b_re, b_im, c_re, c_im, seg_len):
    hp = lax.Precision.HIGHEST
    r = S5_BLOCK
    g_n, p_n, h_n = S5_GROUPS, S5_STATE, S5_GROUP
    lr = jnp.minimum(lam_re.astype(F32), -1e-4)
    li = lam_im.astype(F32)
    dt = jnp.exp(log_dt.astype(F32))[:, None]
    mag = jnp.exp(lr * dt)
    ab_re = mag * jnp.cos(li * dt)
    ab_im = mag * jnp.sin(li * dt)
    den = lr * lr + li * li
    n_re = ab_re - 1.0
    n_im = ab_im
    z_re = (n_re * lr + n_im * li) / den
    z_im = (n_im * lr - n_re * li) / den
    br = b_re.astype(F32)
    bi = b_im.astype(F32)
    bb_re = z_re[..., None] * br - z_im[..., None] * bi
    bb_im = z_re[..., None] * bi + z_im[..., None] * br

    ldt = (lr * dt)[:, None, :]
    wdt = (li * dt)[:, None, :]

    def a_pow(k):
        pm = jnp.exp(ldt * k)
        return pm * jnp.cos(wdt * k), pm * jnp.sin(wdt * k)

    pr, pi = a_pow(jnp.arange(r + 1, dtype=F32)[None, :, None])
    cr = c_re.astype(F32)
    ci = c_im.astype(F32)

    car_re = cr[:, :, None, :] * pr[:, None, :r] - ci[:, :, None, :] * pi[:, None, :r]
    car_im = cr[:, :, None, :] * pi[:, None, :r] + ci[:, :, None, :] * pr[:, None, :r]
    kt = (jnp.einsum('galp,gph->galh', car_re, bb_re, precision=hp)
          - jnp.einsum('galp,gph->galh', car_im, bb_im, precision=hp))
    kw = kt.reshape(g_n, h_n, r * h_n).astype(BF16)

    cre = (cr[:, None] * pr[:, 1:, None, :] - ci[:, None] * pi[:, 1:, None, :]).reshape(
        g_n, r * h_n, p_n)
    cim = -(cr[:, None] * pi[:, 1:, None, :] + ci[:, None] * pr[:, 1:, None, :]).reshape(
        g_n, r * h_n, p_n)
    cp = jnp.concatenate([cre, cim], axis=2).astype(BF16)

    bbr = jnp.transpose(bb_re, (0, 2, 1))[:, None]
    bbi = jnp.transpose(bb_im, (0, 2, 1))[:, None]
    prr = pr[:, r - 1::-1][:, :, None, :]
    pir = pi[:, r - 1::-1][:, :, None, :]
    bpg = jnp.concatenate([(prr * bbr - pir * bbi).reshape(g_n, r * h_n, p_n),
                           (prr * bbi + pir * bbr).reshape(g_n, r * h_n, p_n)],
                          axis=2).astype(BF16)

    at_re, at_im = a_pow(float(r * seg_len))
    n_chunks = g_n // S5_CHUNK_GROUPS
    a_tab = jnp.stack([x.reshape(n_chunks, S5_CHUNK_LANES)
                       for x in (pr[:, r], pi[:, r], at_re, at_im)], axis=1)
    return kw, cp, bpg, a_tab


def _s5_placement_tables():
    r, h_n, p_n = S5_BLOCK, S5_GROUP, S5_STATE
    pcon = np.zeros((r, r, h_n, r, h_n), np.float32)
    for jo in range(r):
        for ji in range(jo + 1):
            pcon[jo, jo - ji, np.arange(h_n), ji, np.arange(h_n)] = 1.0
    econ = np.zeros((2, 2 * p_n, 4 * p_n), np.float32)
    for par in range(2):
        econ[par, np.arange(p_n), par * p_n + np.arange(p_n)] = 1.0
        econ[par, p_n + np.arange(p_n), (2 + par) * p_n + np.arange(p_n)] = 1.0
    return (jnp.asarray(pcon.reshape(r, r * h_n, r * h_n), BF16), jnp.asarray(econ, BF16))


def _s5_branch(u, kw, cp, bpg, a_tab, bsz, seq):
    n_blk = seq // S5_BLOCK
    seg_len = n_blk // S5_SEGS
    width = S5_BLOCK * S5_GROUP
    slab = S5_CHUNK_GROUPS * S5_GROUP
    scan_rows = S5_SEGS * S5_SEG_PITCH
    assert slab == LANES and S5_GROUPS // S5_CHUNK_GROUPS == S5_SLABS
    pcon, econ = _s5_placement_tables()
    tok = pl.BlockSpec((None, seq, slab), lambda c, b: (c, b, 0))

    def per_chunk(shape):
        return pl.BlockSpec((S5_CHUNK_GROUPS,) + shape, lambda c, b: (c, 0, 0))

    return pl.pallas_call(
        functools.partial(_s5_kernel, n_blk=n_blk, seg_len=seg_len),
        grid=(S5_GROUPS // S5_CHUNK_GROUPS, bsz),
        in_specs=[tok, per_chunk((S5_GROUP, width)), per_chunk((width, 2 * S5_STATE)),
                  per_chunk((width, 2 * S5_STATE)), _const_spec(pcon.shape), _const_spec(econ.shape),
                  pl.BlockSpec((None, 4, S5_CHUNK_LANES), lambda c, b: (c, 0, 0))],
        out_specs=tok,
        out_shape=jax.ShapeDtypeStruct(u.shape, BF16),
        scratch_shapes=[
            pltpu.VMEM((n_blk * S5_TOK_PITCH, slab), F32),
            pltpu.VMEM((S5_CHUNK_GROUPS, width, n_blk), BF16),
            pltpu.VMEM((S5_CHUNK_GROUPS, width, n_blk), F32),
            pltpu.VMEM((2 * S5_CHUNK_PAIRS, scan_rows, LANES), F32),
            pltpu.VMEM((2 * S5_CHUNK_PAIRS, scan_rows, LANES), F32),
            pltpu.VMEM((S5_SEGS, 2 * S5_CHUNK_LANES), F32),
            pltpu.VMEM((S5_SEGS, 2 * S5_CHUNK_LANES), F32),
            pltpu.VMEM((S5_CHUNK_GROUPS, width, width + 2 * S5_STATE), BF16),
            pltpu.VMEM((S5_CHUNK_PAIRS, 2 * width, 4 * S5_STATE), BF16),
        ],
        compiler_params=pltpu.CompilerParams(
            dimension_semantics=("arbitrary", "arbitrary"), vmem_limit_bytes=VMEM_LIMIT_BYTES),
        name="s5_branch",
    )(u, kw, cp, bpg, pcon, econ, a_tab)


def _retention_kernel(q_ref, k_ref, kz_ref, v_ref, bz_ref, gain_ref, decay_ref, xi_ref,
                      yb_ref, state, *, chunk_decay):
    c = pl.program_id(1)

    @pl.when(c == 0)
    def _():
        state[...] = jnp.zeros_like(state)

    for ck in range(RET_STEP_CHUNKS):
        rows = slice(ck * RET_CHUNK, (ck + 1) * RET_CHUNK)
        for h in range(RET_HEADS):
            cols = slice(h * RET_DK, (h + 1) * RET_DK)
            qh = q_ref[rows, cols]
            vh = v_ref[rows, cols]
            scores = lax.dot_general(qh, k_ref[rows, cols], (((1,), (1,)), ((), ())),
                                     preferred_element_type=F32)
            inner = jnp.dot((scores * decay_ref[h]).astype(BF16), vh, preferred_element_type=F32)
            prev = state[h]
            cross = jnp.dot(qh, prev.astype(BF16), preferred_element_type=F32) * xi_ref[:, cols]
            local = lax.dot_general(kz_ref[rows, cols], vh, (((0,), (0,)), ((), ())),
                                    preferred_element_type=F32)
            state[h] = prev * chunk_decay[h] + local
            o = inner + cross
            mu = jnp.mean(o, axis=-1, keepdims=True)
            oc = o - mu
            var = jnp.mean(oc * oc, axis=-1, keepdims=True)
            o = oc * lax.rsqrt(var + NORM_EPS) * gain_ref[:, cols]
            yb_ref[rows, cols] = (o * bz_ref[rows, cols].astype(F32)).astype(BF16)


def _retention_tables():
    log_g = np.log1p(-np.exp2(-5.0 - np.arange(RET_HEADS, dtype=np.float64)))
    idx = np.arange(RET_CHUNK, dtype=np.float64)
    diff = idx[:, None] - idx[None, :]
    decay = np.where(diff >= 0, np.exp(log_g[:, None, None] * np.maximum(diff, 0.0)), 0.0)
    xi = np.repeat(np.exp(log_g[None, :] * (idx[:, None] + 1.0)), RET_DK, axis=1)
    zeta = np.repeat(np.exp(log_g[None, :] * (RET_CHUNK - 1.0 - idx[:, None])), RET_DK, axis=1)
    zeta = np.tile(zeta, (PROJ_TILE // RET_CHUNK, 1))
    chunk_decay = tuple(float(x) for x in np.exp(log_g * RET_CHUNK))
    return (jnp.asarray(decay, F32), jnp.asarray(xi, F32), jnp.asarray(zeta, F32), chunk_decay)


def _retention_branch(q, k, kz, v, bz, gain, decay, xi, chunk_decay, bsz, seq):
    step = RET_STEP_CHUNKS * RET_CHUNK
    steps = seq // step
    tok = pl.BlockSpec((step, D_MODEL), lambda b, c: (b * steps + c, 0))
    return pl.pallas_call(
        functools.partial(_retention_kernel, chunk_decay=chunk_decay),
        grid=(bsz, steps),
        in_specs=[tok, tok, tok, tok, tok, _const_spec((1, D_MODEL), False),
                  _const_spec(decay.shape, False), _const_spec(xi.shape, False)],
        out_specs=tok,
        out_shape=jax.ShapeDtypeStruct(q.shape, BF16),
        scratch_shapes=[pltpu.VMEM((RET_HEADS, RET_DK, RET_DV), F32)],
        compiler_params=pltpu.CompilerParams(
            dimension_semantics=("arbitrary", "arbitrary"), vmem_limit_bytes=VMEM_LIMIT_BYTES),
        name="retention_branch",
    )(q, k, kz, v, bz, gain, decay, xi)


def _even_out_kernel(x_ref, ys_ref, u_ref, az_ref, yb_ref, d_ref, wglu_ref, bglu_ref, w_ref,
                     o_ref):
    ys = jnp.concatenate([ys_ref[s] for s in range(S5_SLABS)], axis=1)
    u = jnp.concatenate([u_ref[s] for s in range(S5_SLABS)], axis=1)
    y = ys.astype(F32) + d_ref[...] * u.astype(F32)
    y = jax.nn.gelu(y)
    gl = jnp.dot(y.astype(BF16), wglu_ref[...], preferred_element_type=F32) + bglu_ref[...]
    ya = (y * jax.nn.sigmoid(gl) * az_ref[...].astype(F32)).astype(BF16)
    acc = jnp.dot(ya, w_ref[0:S5_WIDTH, :], preferred_element_type=F32)
    acc = acc + jnp.dot(yb_ref[...], w_ref[S5_WIDTH:, :], preferred_element_type=F32)
    o_ref[...] = x_ref[...] + acc


def _even_out_proj(x2d, ys, u, az, yb, d_skip, w_glu, b_glu, w_out):
    n = x2d.shape[0]
    tm = OUT_TILE
    tok = pl.BlockSpec((tm, D_MODEL), lambda i: (i, 0))
    slabs = pl.BlockSpec((S5_SLABS, tm, LANES), lambda i: (0, i, 0))
    return pl.pallas_call(
        _even_out_kernel,
        grid=(n // tm,),
        in_specs=[tok, slabs, slabs, tok, tok, _const_spec((1, S5_WIDTH), False),
                  _const_spec(w_glu.shape, False), _const_spec((1, S5_WIDTH), False),
                  _const_spec(w_out.shape, False)],
        out_specs=tok,
        out_shape=jax.ShapeDtypeStruct(x2d.shape, F32),
        compiler_params=pltpu.CompilerParams(
            dimension_semantics=("arbitrary",), vmem_limit_bytes=VMEM_LIMIT_BYTES),
        name="even_out_proj",
    )(x2d, ys, u, az, yb, d_skip, w_glu, b_glu, w_out)


def _odd_kernel(x_ref, g_ref, w_ref, vgain_ref, wsp_ref, bsp_ref, wout_ref, fg_ref, o_ref,
                vbuf, ubuf, gbuf, vnbuf, ybuf):
    x = x_ref[...]
    hn = _rms_norm(x, g_ref[...]).astype(BF16)
    gd = SGU_GROUP_DIM

    def proj(col):
        return jnp.dot(hn, w_ref[:, col:col + gd], preferred_element_type=F32)

    for g in range(SGU_GROUPS):
        vbuf[:, g * gd:(g + 1) * gd] = jax.nn.gelu(proj(SGU_WIDTH + g * gd))
    for g in range(SGU_GROUPS):
        ubuf[:, g * gd:(g + 1) * gd] = jax.nn.gelu(proj(g * gd))
        gbuf[:, g * gd:(g + 1) * gd] = _silu(proj(2 * SGU_WIDTH + g * gd))
    v = vbuf[...]
    mu = jnp.mean(v, axis=-1, keepdims=True)
    vc = v - mu
    var = jnp.mean(vc * vc, axis=-1, keepdims=True)
    vnbuf[...] = (vc * lax.rsqrt(var + NORM_EPS) * vgain_ref[...]).astype(BF16)

    row = lax.broadcasted_iota(jnp.int32, (SGU_CHUNK, SGU_CHUNK), 0)
    col = lax.broadcasted_iota(jnp.int32, (SGU_CHUNK, SGU_CHUNK), 1)
    for g in range(SGU_GROUPS):
        cols = slice(g * gd, (g + 1) * gd)
        wm = jnp.where(row >= col, wsp_ref[g], 0.0).astype(BF16)
        bias = bsp_ref[g]
        for c in range(ODD_TILE // SGU_CHUNK):
            rows = slice(c * SGU_CHUNK, (c + 1) * SGU_CHUNK)
            s = jnp.dot(wm, vnbuf[rows, cols], preferred_element_type=F32) + bias
            ybuf[rows, cols] = (ubuf[rows, cols] * s * gbuf[rows, cols]).astype(BF16)

    xn = x + jnp.dot(ybuf[...], wout_ref[...], preferred_element_type=F32)
    o_ref[...] = _rms_norm(xn, fg_ref[...])


def _odd_layer(x2d, gain, w_in, vgain, wsp, bsp, w_out, final_gain):
    n = x2d.shape[0]
    tm = ODD_TILE
    tok = pl.BlockSpec((tm, D_MODEL), lambda i: (i, 0))
    return pl.pallas_call(
        _odd_kernel,
        grid=(n // tm,),
        in_specs=[tok, _const_spec((1, D_MODEL)), _const_spec(w_in.shape),
                  _const_spec((1, SGU_WIDTH)), _const_spec(wsp.shape), _const_spec(bsp.shape),
                  _const_spec(w_out.shape), _const_spec((1, D_MODEL))],
        out_specs=tok,
        out_shape=jax.ShapeDtypeStruct(x2d.shape, F32),
        scratch_shapes=[pltpu.VMEM((tm, SGU_WIDTH), F32),
                        pltpu.VMEM((tm, SGU_WIDTH), F32),
                        pltpu.VMEM((tm, SGU_WIDTH), F32),
                        pltpu.VMEM((tm, SGU_WIDTH), BF16),
                        pltpu.VMEM((tm, SGU_WIDTH), BF16)],
        compiler_params=pltpu.CompilerParams(
            dimension_semantics=("arbitrary",), vmem_limit_bytes=VMEM_LIMIT_BYTES),
        name="odd_layer",
    )(x2d, gain, w_in, vgain, wsp, bsp, w_out, final_gain)


def _rotary_tables(seq):
    half = RET_DK // 2
    pos = np.arange(seq, dtype=np.float64)
    inv = ROPE_BASE ** (-np.arange(half, dtype=np.float64) / half)
    ang = pos[:, None] * inv[None, :]
    return jnp.asarray(np.cos(ang), F32), jnp.asarray(np.sin(ang), F32)


def kernel(x, norm_even, w_in_even, s5_lam_re, s5_lam_im, s5_log_dt, s5_b_re, s5_b_im, s5_c_re, s5_c_im, s5_d, s5_w_glu, s5_b_glu, ret_gn_gain, w_out_even, norm_odd, w_in_odd, sgu_norm_gain, sgu_w_spatial, sgu_b_spatial, w_out_odd, final_norm):
    bsz, seq, d = x.shape
    x2d = x.reshape(bsz * seq, d)
    cos, sin = _rotary_tables(seq)

    decay, xi, zeta, chunk_decay = _retention_tables()
    u, az, q, k, kz, v, bz = _even_in_proj(
        x2d, norm_even[0].reshape(1, d), w_in_even[0].astype(BF16), cos, sin, zeta, seq)
    seg_len = seq // S5_BLOCK // S5_SEGS
    kw, cp, bpg, a_tab = _s5_params(s5_lam_re[0], s5_lam_im[0], s5_log_dt[0], s5_b_re[0],
                                    s5_b_im[0], s5_c_re[0], s5_c_im[0], seg_len)
    ys = _s5_branch(u, kw, cp, bpg, a_tab, bsz, seq)
    yb = _retention_branch(q, k, kz, v, bz, ret_gn_gain[0].reshape(1, -1).astype(F32),
                           decay, xi, chunk_decay, bsz, seq)
    x2d = _even_out_proj(x2d, ys, u, az, yb, s5_d[0].reshape(1, -1).astype(F32),
                         s5_w_glu[0].astype(BF16), s5_b_glu[0].reshape(1, -1).astype(F32),
                         w_out_even[0].astype(BF16))

    out = _odd_layer(x2d, norm_odd[0].reshape(1, d), w_in_odd[0].astype(BF16),
                     sgu_norm_gain[0].reshape(1, -1).astype(F32), sgu_w_spatial[0].astype(F32),
                     sgu_b_spatial[0].astype(F32)[:, :, None], w_out_odd[0].astype(BF16),
                     final_norm.reshape(1, d))
    return out.reshape(bsz, seq, d)
```

```python
import functools

import jax
import jax.numpy as jnp
import numpy as np
from jax import lax
from jax.experimental import pallas as pl
from jax.experimental.pallas import tpu as pltpu

F32 = jnp.float32
BF16 = jnp.bfloat16

D_MODEL = 1024
S5_WIDTH = 1024
S5_GROUP = 16
S5_GROUPS = 64
S5_STATE = 64
RET_HEADS = 4
RET_DK = 256
RET_DV = 256
ROPE_BASE = 10000.0
SGU_WIDTH = 2048
SGU_GROUPS = 4
SGU_GROUP_DIM = 512
SGU_CHUNK = 128
NORM_EPS = 1e-6

V7X_VMEM_BYTES = 64 * 1024 * 1024
VMEM_LIMIT_BYTES = V7X_VMEM_BYTES - 12 * 1024 * 1024
SUBLANES = 8
LANES = 128

PROJ_TILE = 512
OUT_TILE = 512
S5_BLOCK = 16
S5_SEGS = SUBLANES
S5_CHUNK_GROUPS = 8
S5_CHUNK_PAIRS = S5_CHUNK_GROUPS // 2
S5_CHUNK_LANES = S5_CHUNK_GROUPS * S5_STATE
S5_SLABS = S5_WIDTH // LANES
S5_SEG_PITCH = 72
S5_TOK_PITCH = 24
RET_CHUNK = 256
RET_STEP_CHUNKS = 4
ODD_TILE = 512


def _const_spec(shape, single_buffer=True):
    zeros = (0,) * len(shape)
    if single_buffer:
        return pl.BlockSpec(shape, lambda *_: zeros, pipeline_mode=pl.Buffered(1))
    return pl.BlockSpec(shape, lambda *_: zeros)


def _rms_norm(x, gain):
    ms = jnp.mean(x * x, axis=-1, keepdims=True)
    return x * lax.rsqrt(ms + NORM_EPS) * gain


def _silu(x):
    return x * jax.nn.sigmoid(x)


def _even_in_kernel(x_ref, g_ref, w_ref, cos_ref, sin_ref, zeta_ref, *rest):
    n_cast = (len(rest) - 7) // 2
    cast_in = rest[:n_cast]
    u_ref, az_ref, q_ref, k_ref, kz_ref, v_ref, bz_ref = rest[n_cast:n_cast + 7]
    cast_out = rest[n_cast + 7:]
    for src, dst in zip(cast_in, cast_out):
        dst[...] = src[...].astype(BF16)

    hn = _rms_norm(x_ref[...], g_ref[...]).astype(BF16)

    def proj(j):
        return jnp.dot(hn, w_ref[:, j * D_MODEL:(j + 1) * D_MODEL], preferred_element_type=F32)

    def rotary_store(p, out_ref, scale, decayed_ref=None):
        cos = cos_ref[...]
        sin = sin_ref[...]
        half = RET_DK // 2
        for h in range(RET_HEADS):
            lo = slice(h * RET_DK, h * RET_DK + half)
            hi = slice(h * RET_DK + half, (h + 1) * RET_DK)
            x1 = p[:, lo]
            x2 = p[:, hi]
            o1 = x1 * cos - x2 * sin
            o2 = x1 * sin + x2 * cos
            if scale != 1.0:
                o1 = o1 * scale
                o2 = o2 * scale
            out_ref[:, lo] = o1.astype(BF16)
            out_ref[:, hi] = o2.astype(BF16)
            if decayed_ref is not None:
                decayed_ref[:, lo] = (o1 * zeta_ref[:, lo]).astype(BF16)
                decayed_ref[:, hi] = (o2 * zeta_ref[:, hi]).astype(BF16)

    u = proj(0).astype(BF16)
    for s in range(S5_SLABS):
        u_ref[s] = u[:, s * LANES:(s + 1) * LANES]
    az_ref[...] = _silu(proj(1)).astype(BF16)
    rotary_store(proj(2), q_ref, 1.0)
    rotary_store(proj(3), k_ref, RET_DK ** -0.5, kz_ref)
    v_ref[...] = proj(4).astype(BF16)
    bz_ref[...] = _silu(proj(5)).astype(BF16)


def _even_in_proj(x2d, gain, w_in, cos, sin, zeta, seq, later_weights):
    n = x2d.shape[0]
    tm = PROJ_TILE
    steps = n // tm
    tiles_per_seq = seq // tm
    tok = pl.BlockSpec((tm, D_MODEL), lambda i: (i, 0))
    rot = pl.BlockSpec((tm, RET_DK // 2), lambda i: (i % tiles_per_seq, 0))
    out = jax.ShapeDtypeStruct((n, D_MODEL), BF16)
    cast_specs = [pl.BlockSpec((w.shape[0] // steps, w.shape[1]), lambda i: (i, 0))
                  for w in later_weights]
    res = pl.pallas_call(
        _even_in_kernel,
        grid=(steps,),
        in_specs=[tok, _const_spec((1, D_MODEL)), _const_spec(w_in.shape), rot, rot,
                  _const_spec(zeta.shape)] + cast_specs,
        out_specs=([pl.BlockSpec((S5_SLABS, tm, LANES), lambda i: (0, i, 0))] + [tok] * 6
                   + cast_specs),
        out_shape=([jax.ShapeDtypeStruct((S5_SLABS, n, LANES), BF16)] + [out] * 6
                   + [jax.ShapeDtypeStruct(w.shape, BF16) for w in later_weights]),
        compiler_params=pltpu.CompilerParams(
            dimension_semantics=("arbitrary",), vmem_limit_bytes=VMEM_LIMIT_BYTES),
        name="even_in_proj",
    )(x2d, gain, w_in, cos, sin, zeta, *later_weights)
    return res[:7], res[7:]


def _s5_kernel(u_ref, kw_ref, cp_ref, bpg_ref, pcon_ref, econ_ref, a_ref, y_ref,
               tokbuf, zt, ytbuf, vbuf, spbuf, ebuf, sinbuf, wyt, bp, *, n_blk, seg_len):
    lanes = S5_CHUNK_LANES
    re = slice(0, lanes)
    im = slice(lanes, 2 * lanes)
    pairs = S5_CHUNK_PAIRS
    blk_w = S5_BLOCK * S5_GROUP

    @pl.when(pl.program_id(1) == 0)
    def _():
        kw = kw_ref[...].reshape(S5_CHUNK_GROUPS * S5_GROUP, blk_w)
        for jo in range(S5_BLOCK):
            rows = jnp.dot(kw, pcon_ref[jo], preferred_element_type=F32).astype(BF16)
            for gl in range(S5_CHUNK_GROUPS):
                wyt[gl, jo * S5_GROUP:(jo + 1) * S5_GROUP, 0:blk_w] = (
                    rows[gl * S5_GROUP:(gl + 1) * S5_GROUP, :])
        for gl in range(S5_CHUNK_GROUPS):
            wyt[gl, :, blk_w:] = cp_ref[gl]
        for q in range(pairs):
            for par in range(2):
                bp[q, par * blk_w:(par + 1) * blk_w, :] = jnp.dot(
                    bpg_ref[2 * q + par], econ_ref[par], preferred_element_type=F32).astype(BF16)

    for b in range(n_blk):
        tokbuf[b * S5_TOK_PITCH:b * S5_TOK_PITCH + S5_BLOCK, :] = (
            u_ref[b * S5_BLOCK:(b + 1) * S5_BLOCK, :].astype(F32))
    for j in range(S5_BLOCK):
        ujt = tokbuf[pl.ds(j, n_blk, stride=S5_TOK_PITCH), :].astype(BF16).T
        for gl in range(S5_CHUNK_GROUPS):
            zt[gl, j * S5_GROUP:(j + 1) * S5_GROUP, :] = ujt[gl * S5_GROUP:(gl + 1) * S5_GROUP, :]

    for q in range(pairs):
        zc = jnp.concatenate([zt[2 * q], zt[2 * q + 1]], axis=0)
        v = lax.dot_general(zc, bp[q], (((0,), (0,)), ((), ())),
                            preferred_element_type=F32)
        for seg in range(S5_SEGS):
            src = slice(seg * seg_len, (seg + 1) * seg_len)
            dst = slice(seg * S5_SEG_PITCH, seg * S5_SEG_PITCH + seg_len)
            vbuf[q, dst, :] = v[src, 0:LANES]
            vbuf[pairs + q, dst, :] = v[src, LANES:]

    ar = jnp.broadcast_to(a_ref[0:1, :], (SUBLANES, lanes))
    ai = jnp.broadcast_to(a_ref[1:2, :], (SUBLANES, lanes))

    def advance(sr, si, i):
        rows = pl.ds(i, S5_SEGS, stride=S5_SEG_PITCH)
        xr = jnp.concatenate([vbuf[s, rows, :] for s in range(pairs)], axis=1)
        xi = jnp.concatenate([vbuf[pairs + s, rows, :] for s in range(pairs)], axis=1)
        return ar * sr - ai * si + xr, ar * si + ai * sr + xi

    def end_state(i, state):
        return advance(*state, i)

    zero = jnp.zeros((SUBLANES, lanes), F32)
    er, ei = lax.fori_loop(0, seg_len, end_state, (zero, zero), unroll=8)
    ebuf[:, re] = er
    ebuf[:, im] = ei

    atr = a_ref[2:3, :]
    ati = a_ref[3:4, :]
    cr = jnp.zeros((1, lanes), F32)
    ci = jnp.zeros((1, lanes), F32)
    for seg in range(S5_SEGS):
        sinbuf[seg:seg + 1, re] = cr
        sinbuf[seg:seg + 1, im] = ci
        cr, ci = (atr * cr - ati * ci + ebuf[seg:seg + 1, re],
                  atr * ci + ati * cr + ebuf[seg:seg + 1, im])

    def record(i, state):
        sr, si = state
        rows = pl.ds(i, S5_SEGS, stride=S5_SEG_PITCH)
        for s in range(pairs):
            spbuf[s, rows, :] = sr[:, s * LANES:(s + 1) * LANES]
            spbuf[pairs + s, rows, :] = si[:, s * LANES:(s + 1) * LANES]
        return advance(sr, si, i)

    lax.fori_loop(0, seg_len, record, (sinbuf[:, re], sinbuf[:, im]), unroll=8)

    def natural_rows(slab):
        return jnp.concatenate(
            [spbuf[slab, seg * S5_SEG_PITCH:seg * S5_SEG_PITCH + seg_len, :]
             for seg in range(S5_SEGS)], axis=0)

    for q in range(pairs):
        sp = jnp.concatenate([natural_rows(q), natural_rows(pairs + q)], axis=1)
        spt = sp.astype(BF16).T
        for par in range(2):
            gl = 2 * q + par
            rhs = jnp.concatenate(
                [zt[gl], spt[par * S5_STATE:(par + 1) * S5_STATE],
                 spt[LANES + par * S5_STATE:LANES + (par + 1) * S5_STATE]], axis=0)
            ytbuf[gl] = jnp.dot(wyt[gl], rhs, preferred_element_type=F32)

    for j in range(S5_BLOCK):
        yt = jnp.concatenate([ytbuf[gl, j * S5_GROUP:(j + 1) * S5_GROUP, :]
                              for gl in range(S5_CHUNK_GROUPS)], axis=0)
        tokbuf[pl.ds(j, n_blk, stride=S5_TOK_PITCH), :] = yt.astype(BF16).T.astype(F32)
    for b in range(n_blk):
        y_ref[b * S5_BLOCK:(b + 1) * S5_BLOCK, :] = (
            tokbuf[b * S5_TOK_PITCH:b * S5_TOK_PITCH + S5_BLOCK, :].astype(BF16))


def _s5_params(lam_re, lam_im, log_dt, b_re, b_im, c_re, c_im, seg_len):
    hp = lax.Precision.HIGHEST
    r = S5_BLOCK
    g_n, p_n, h_n = S5_GROUPS, S5_STATE, S5_GROUP
    lr = jnp.minimum(lam_re.astype(F32), -1e-4)
    li = lam_im.astype(F32)
    dt = jnp.exp(log_dt.astype(F32))[:, None]
    mag = jnp.exp(lr * dt)
    ab_re = mag * jnp.cos(li * dt)
    ab_im = mag * jnp.sin(li * dt)
    den = lr * lr + li * li
    n_re = ab_re - 1.0
    n_im = ab_im
    z_re = (n_re * lr + n_im * li) / den
    z_im = (n_im * lr - n_re * li) / den
    br = b_re.astype(F32)
    bi = b_im.astype(F32)
    bb_re = z_re[..., None] * br - z_im[..., None] * bi
    bb_im = z_re[..., None] * bi + z_im[..., None] * br

    ldt = (lr * dt)[:, None, :]
    wdt = (li * dt)[:, None, :]

    def a_pow(k):
        pm = jnp.exp(ldt * k)
        return pm * jnp.cos(wdt * k), pm * jnp.sin(wdt * k)

    pr, pi = a_pow(jnp.arange(r + 1, dtype=F32)[None, :, None])
    cr = c_re.astype(F32)
    ci = c_im.astype(F32)

    car_re = cr[:, :, None, :] * pr[:, None, :r] - ci[:, :, None, :] * pi[:, None, :r]
    car_im = cr[:, :, None, :] * pi[:, None, :r] + ci[:, :, None, :] * pr[:, None, :r]
    kt = (jnp.einsum('galp,gph->galh', car_re, bb_re, precision=hp)
          - jnp.einsum('galp,gph->galh', car_im, bb_im, precision=hp))
    kw = kt.reshape(g_n, h_n, r * h_n).astype(BF16)

    cre = (cr[:, None] * pr[:, 1:, None, :] - ci[:, None] * pi[:, 1:, None, :]).reshape(
        g_n, r * h_n, p_n)
    cim = -(cr[:, None] * pi[:, 1:, None, :] + ci[:, None] * pr[:, 1:, None, :]).reshape(
        g_n, r * h_n, p_n)
    cp = jnp.concatenate([cre, cim], axis=2).astype(BF16)

    bbr = jnp.transpose(bb_re, (0, 2, 1))[:, None]
    bbi = jnp.transpose(bb_im, (0, 2, 1))[:, None]
    prr = pr[:, r - 1::-1][:, :, None, :]
    pir = pi[:, r - 1::-1][:, :, None, :]
    bpg = jnp.concatenate([(prr * bbr - pir * bbi).reshape(g_n, r * h_n, p_n),
                           (prr * bbi + pir * bbr).reshape(g_n, r * h_n, p_n)],
                          axis=2).astype(BF16)

    at_re, at_im = a_pow(float(r * seg_len))
    n_chunks = g_n // S5_CHUNK_GROUPS
    a_tab = jnp.stack([x.reshape(n_chunks, S5_CHUNK_LANES)
                       for x in (pr[:, r], pi[:, r], at_re, at_im)], axis=1)
    return kw, cp, bpg, a_tab


def _s5_placement_tables():
    r, h_n, p_n = S5_BLOCK, S5_GROUP, S5_STATE
    pcon = np.zeros((r, r, h_n, r, h_n), np.float32)
    for jo in range(r):
        for ji in range(jo + 1):
            pcon[jo, jo - ji, np.arange(h_n), ji, np.arange(h_n)] = 1.0
    econ = np.zeros((2, 2 * p_n, 4 * p_n), np.float32)
    for par in range(2):
        econ[par, np.arange(p_n), par * p_n + np.arange(p_n)] = 1.0
        econ[par, p_n + np.arange(p_n), (2 + par) * p_n + np.arange(p_n)] = 1.0
    return (jnp.asarray(pcon.reshape(r, r * h_n, r * h_n), BF16), jnp.asarray(econ, BF16))


def _s5_branch(u, kw, cp, bpg, a_tab, bsz, seq):
    n_blk = seq // S5_BLOCK
    seg_len = n_blk // S5_SEGS
    width = S5_BLOCK * S5_GROUP
    slab = S5_CHUNK_GROUPS * S5_GROUP
    scan_rows = S5_SEGS * S5_SEG_PITCH
    assert slab == LANES and S5_GROUPS // S5_CHUNK_GROUPS == S5_SLABS
    pcon, econ = _s5_placement_tables()
    tok = pl.BlockSpec((None, seq, slab), lambda c, b: (c, b, 0))

    def per_chunk(shape):
        return pl.BlockSpec((S5_CHUNK_GROUPS,) + shape, lambda c, b: (c, 0, 0))

    return pl.pallas_call(
        functools.partial(_s5_kernel, n_blk=n_blk, seg_len=seg_len),
        grid=(S5_GROUPS // S5_CHUNK_GROUPS, bsz),
        in_specs=[tok, per_chunk((S5_GROUP, width)), per_chunk((width, 2 * S5_STATE)),
                  per_chunk((width, 2 * S5_STATE)), _const_spec(pcon.shape), _const_spec(econ.shape),
                  pl.BlockSpec((None, 4, S5_CHUNK_LANES), lambda c, b: (c, 0, 0))],
        out_specs=tok,
        out_shape=jax.ShapeDtypeStruct(u.shape, BF16),
        scratch_shapes=[
            pltpu.VMEM((n_blk * S5_TOK_PITCH, slab), F32),
            pltpu.VMEM((S5_CHUNK_GROUPS, width, n_blk), BF16),
            pltpu.VMEM((S5_CHUNK_GROUPS, width, n_blk), F32),
            pltpu.VMEM((2 * S5_CHUNK_PAIRS, scan_rows, LANES), F32),
            pltpu.VMEM((2 * S5_CHUNK_PAIRS, scan_rows, LANES), F32),
            pltpu.VMEM((S5_SEGS, 2 * S5_CHUNK_LANES), F32),
            pltpu.VMEM((S5_SEGS, 2 * S5_CHUNK_LANES), F32),
            pltpu.VMEM((S5_CHUNK_GROUPS, width, width + 2 * S5_STATE), BF16),
            pltpu.VMEM((S5_CHUNK_PAIRS, 2 * width, 4 * S5_STATE), BF16),
        ],
        compiler_params=pltpu.CompilerParams(
            dimension_semantics=("arbitrary", "arbitrary"), vmem_limit_bytes=VMEM_LIMIT_BYTES),
        name="s5_branch",
    )(u, kw, cp, bpg, pcon, econ, a_tab)


def _retention_kernel(q_ref, k_ref, kz_ref, v_ref, bz_ref, gain_ref, decay_ref, xi_ref,
                      yb_ref, state, *, chunk_decay):
    c = pl.program_id(1)

    @pl.when(c == 0)
    def _():
        state[...] = jnp.zeros_like(state)

    for ck in range(RET_STEP_CHUNKS):
        rows = slice(ck * RET_CHUNK, (ck + 1) * RET_CHUNK)
        for h in range(RET_HEADS):
            cols = slice(h * RET_DK, (h + 1) * RET_DK)
            qh = q_ref[rows, cols]
            vh = v_ref[rows, cols]
            scores = lax.dot_general(qh, k_ref[rows, cols], (((1,), (1,)), ((), ())),
                                     preferred_element_type=F32)
            inner = jnp.dot((scores * decay_ref[h]).astype(BF16), vh, preferred_element_type=F32)
            prev = state[h]
            cross = jnp.dot(qh, prev.astype(BF16), preferred_element_type=F32) * xi_ref[:, cols]
            local = lax.dot_general(kz_ref[rows, cols], vh, (((0,), (0,)), ((), ())),
                                    preferred_element_type=F32)
            state[h] = prev * chunk_decay[h] + local
            o = inner + cross
            mu = jnp.mean(o, axis=-1, keepdims=True)
            oc = o - mu
            var = jnp.mean(oc * oc, axis=-1, keepdims=True)
            o = oc * lax.rsqrt(var + NORM_EPS) * gain_ref[:, cols]
            yb_ref[rows, cols] = (o * bz_ref[rows, cols].astype(F32)).astype(BF16)


def _retention_tables():
    log_g = np.log1p(-np.exp2(-5.0 - np.arange(RET_HEADS, dtype=np.float64)))
    idx = np.arange(RET_CHUNK, dtype=np.float64)
    diff = idx[:, None] - idx[None, :]
    decay = np.where(diff >= 0, np.exp(log_g[:, None, None] * np.maximum(diff, 0.0)), 0.0)
    xi = np.repeat(np.exp(log_g[None, :] * (idx[:, None] + 1.0)), RET_DK, axis=1)
    zeta = np.repeat(np.exp(log_g[None, :] * (RET_CHUNK - 1.0 - idx[:, None])), RET_DK, axis=1)
    zeta = np.tile(zeta, (PROJ_TILE // RET_CHUNK, 1))
    chunk_decay = tuple(float(x) for x in np.exp(log_g * RET_CHUNK))
    return (jnp.asarray(decay, F32), jnp.asarray(xi, F32), jnp.asarray(zeta, F32), chunk_decay)


def _retention_branch(q, k, kz, v, bz, gain, decay, xi, chunk_decay, bsz, seq):
    step = RET_STEP_CHUNKS * RET_CHUNK
    steps = seq // step
    tok = pl.BlockSpec((step, D_MODEL), lambda b, c: (b * steps + c, 0))
    return pl.pallas_call(
        functools.partial(_retention_kernel, chunk_decay=chunk_decay),
        grid=(bsz, steps),
        in_specs=[tok, tok, tok, tok, tok, _const_spec((1, D_MODEL), False),
                  _const_spec(decay.shape, False), _const_spec(xi.shape, False)],
        out_specs=tok,
        out_shape=jax.ShapeDtypeStruct(q.shape, BF16),
        scratch_shapes=[pltpu.VMEM((RET_HEADS, RET_DK, RET_DV), F32)],
        compiler_params=pltpu.CompilerParams(
            dimension_semantics=("arbitrary", "arbitrary"), vmem_limit_bytes=VMEM_LIMIT_BYTES),
        name="retention_branch",
    )(q, k, kz, v, bz, gain, decay, xi)


def _even_out_kernel(x_ref, ys_ref, u_ref, az_ref, yb_ref, d_ref, wglu_ref, bglu_ref, w_ref,
                     o_ref):
    ys = jnp.concatenate([ys_ref[s] for s in range(S5_SLABS)], axis=1)
    u = jnp.concatenate([u_ref[s] for s in range(S5_SLABS)], axis=1)
    y = ys.astype(F32) + d_ref[...] * u.astype(F32)
    y = jax.nn.gelu(y)
    gl = jnp.dot(y.astype(BF16), wglu_ref[...], preferred_element_type=F32) + bglu_ref[...]
    ya = (y * jax.nn.sigmoid(gl) * az_ref[...].astype(F32)).astype(BF16)
    acc = jnp.dot(ya, w_ref[0:S5_WIDTH, :], preferred_element_type=F32)
    acc = acc + jnp.dot(yb_ref[...], w_ref[S5_WIDTH:, :], preferred_element_type=F32)
    o_ref[...] = x_ref[...] + acc


def _even_out_proj(x2d, ys, u, az, yb, d_skip, w_glu, b_glu, w_out):
    n = x2d.shape[0]
    tm = OUT_TILE
    tok = pl.BlockSpec((tm, D_MODEL), lambda i: (i, 0))
    slabs = pl.BlockSpec((S5_SLABS, tm, LANES), lambda i: (0, i, 0))
    return pl.pallas_call(
        _even_out_kernel,
        grid=(n // tm,),
        in_specs=[tok, slabs, slabs, tok, tok, _const_spec((1, S5_WIDTH), False),
                  _const_spec(w_glu.shape, False), _const_spec((1, S5_WIDTH), False),
                  _const_spec(w_out.shape, False)],
        out_specs=tok,
        out_shape=jax.ShapeDtypeStruct(x2d.shape, F32),
        compiler_params=pltpu.CompilerParams(
            dimension_semantics=("arbitrary",), vmem_limit_bytes=VMEM_LIMIT_BYTES),
        name="even_out_proj",
    )(x2d, ys, u, az, yb, d_skip, w_glu, b_glu, w_out)


def _odd_kernel(x_ref, g_ref, w_ref, vgain_ref, wsp_ref, bsp_ref, wout_ref, fg_ref, o_ref,
                vbuf, ubuf, gbuf, vnbuf, ybuf):
    x = x_ref[...]
    hn = _rms_norm(x, g_ref[...]).astype(BF16)
    gd = SGU_GROUP_DIM

    def proj(col):
        return jnp.dot(hn, w_ref[:, col:col + gd], preferred_element_type=F32)

    for g in range(SGU_GROUPS):
        vbuf[:, g * gd:(g + 1) * gd] = jax.nn.gelu(proj(SGU_WIDTH + g * gd))
    for g in range(SGU_GROUPS):
        ubuf[:, g * gd:(g + 1) * gd] = jax.nn.gelu(proj(g * gd))
        gbuf[:, g * gd:(g + 1) * gd] = _silu(proj(2 * SGU_WIDTH + g * gd))
    v = vbuf[...]
    mu = jnp.mean(v, axis=-1, keepdims=True)
    vc = v - mu
    var = jnp.mean(vc * vc, axis=-1, keepdims=True)
    vnbuf[...] = (vc * lax.rsqrt(var + NORM_EPS) * vgain_ref[...]).astype(BF16)

    row = lax.broadcasted_iota(jnp.int32, (SGU_CHUNK, SGU_CHUNK), 0)
    col = lax.broadcasted_iota(jnp.int32, (SGU_CHUNK, SGU_CHUNK), 1)
    for g in range(SGU_GROUPS):
        cols = slice(g * gd, (g + 1) * gd)
        wm = jnp.where(row >= col, wsp_ref[g], 0.0).astype(BF16)
        bias = bsp_ref[g]
        for c in range(ODD_TILE // SGU_CHUNK):
            rows = slice(c * SGU_CHUNK, (c + 1) * SGU_CHUNK)
            s = jnp.dot(wm, vnbuf[rows, cols], preferred_element_type=F32) + bias
            ybuf[rows, cols] = (ubuf[rows, cols] * s * gbuf[rows, cols]).astype(BF16)

    xn = x + jnp.dot(ybuf[...], wout_ref[...], preferred_element_type=F32)
    o_ref[...] = _rms_norm(xn, fg_ref[...])


def _odd_layer(x2d, gain, w_in, vgain, wsp, bsp, w_out, final_gain):
    n = x2d.shape[0]
    tm = ODD_TILE
    tok = pl.BlockSpec((tm, D_MODEL), lambda i: (i, 0))
    return pl.pallas_call(
        _odd_kernel,
        grid=(n // tm,),
        in_specs=[tok, _const_spec((1, D_MODEL)), _const_spec(w_in.shape),
                  _const_spec((1, SGU_WIDTH)), _const_spec(wsp.shape), _const_spec(bsp.shape),
                  _const_spec(w_out.shape), _const_spec((1, D_MODEL))],
        out_specs=tok,
        out_shape=jax.ShapeDtypeStruct(x2d.shape, F32),
        scratch_shapes=[pltpu.VMEM((tm, SGU_WIDTH), F32),
                        pltpu.VMEM((tm, SGU_WIDTH), F32),
                        pltpu.VMEM((tm, SGU_WIDTH), F32),
                        pltpu.VMEM((tm, SGU_WIDTH), BF16),
                        pltpu.VMEM((tm, SGU_WIDTH), BF16)],
        compiler_params=pltpu.CompilerParams(
            dimension_semantics=("arbitrary",), vmem_limit_bytes=VMEM_LIMIT_BYTES),
        name="odd_layer",
    )(x2d, gain, w_in, vgain, wsp, bsp, w_out, final_gain)


def _rotary_tables(seq):
    half = RET_DK // 2
    pos = np.arange(seq, dtype=np.float64)
    inv = ROPE_BASE ** (-np.arange(half, dtype=np.float64) / half)
    ang = pos[:, None] * inv[None, :]
    return jnp.asarray(np.cos(ang), F32), jnp.asarray(np.sin(ang), F32)


def kernel(x, norm_even, w_in_even, s5_lam_re, s5_lam_im, s5_log_dt, s5_b_re, s5_b_im, s5_c_re, s5_c_im, s5_d, s5_w_glu, s5_b_glu, ret_gn_gain, w_out_even, norm_odd, w_in_odd, sgu_norm_gain, sgu_w_spatial, sgu_b_spatial, w_out_odd, final_norm):
    bsz, seq, d = x.shape
    x2d = x.reshape(bsz * seq, d)
    cos, sin = _rotary_tables(seq)

    decay, xi, zeta, chunk_decay = _retention_tables()
    (u, az, q, k, kz, v, bz), (w_glu, w_out_e, w_in_o, w_out_o) = _even_in_proj(
        x2d, norm_even[0].reshape(1, d), w_in_even[0].astype(BF16), cos, sin, zeta, seq,
        [s5_w_glu[0], w_out_even[0], w_in_odd[0], w_out_odd[0]])
    seg_len = seq // S5_BLOCK // S5_SEGS
    kw, cp, bpg, a_tab = _s5_params(s5_lam_re[0], s5_lam_im[0], s5_log_dt[0], s5_b_re[0],
                                    s5_b_im[0], s5_c_re[0], s5_c_im[0], seg_len)
    ys = _s5_branch(u, kw, cp, bpg, a_tab, bsz, seq)
    yb = _retention_branch(q, k, kz, v, bz, ret_gn_gain[0].reshape(1, -1).astype(F32),
                           decay, xi, chunk_decay, bsz, seq)
    x2d = _even_out_proj(x2d, ys, u, az, yb, s5_d[0].reshape(1, -1).astype(F32),
                         w_glu, s5_b_glu[0].reshape(1, -1).astype(F32), w_out_e)

    out = _odd_layer(x2d, norm_odd[0].reshape(1, d), w_in_o,
                     sgu_norm_gain[0].reshape(1, -1).astype(F32), sgu_w_spatial[0].astype(F32),
                     sgu_b_spatial[0].astype(F32)[:, :, None], w_out_o,
                     final_norm.reshape(1, d))
    return out.reshape(bsz, seq, d)
```

```python
import functools

import jax
import jax.numpy as jnp
import numpy as np
from jax import lax
from jax.experimental import pallas as pl
from jax.experimental.pallas import tpu as pltpu

F32 = jnp.float32
BF16 = jnp.bfloat16

D_MODEL = 1024
S5_WIDTH = 1024
S5_GROUP = 16
S5_GROUPS = 64
S5_STATE = 64
RET_HEADS = 4
RET_DK = 256
RET_DV = 256
ROPE_BASE = 10000.0
SGU_WIDTH = 2048
SGU_GROUPS = 4
SGU_GROUP_DIM = 512
SGU_CHUNK = 128
NORM_EPS = 1e-6

V7X_VMEM_BYTES = 64 * 1024 * 1024
VMEM_LIMIT_BYTES = V7X_VMEM_BYTES - 12 * 1024 * 1024
SUBLANES = 8
LANES = 128

PROJ_TILE = 512
OUT_TILE = 512
S5_BLOCK = 16
S5_SEGS = SUBLANES
S5_CHUNK_GROUPS = 8
S5_CHUNK_PAIRS = S5_CHUNK_GROUPS // 2
S5_CHUNK_LANES = S5_CHUNK_GROUPS * S5_STATE
S5_SLABS = S5_WIDTH // LANES
S5_SEG_PITCH = 72
S5_TOK_PITCH = 24
RET_CHUNK = 256
ODD_TILE = 512


def _const_spec(shape, single_buffer=True):
    zeros = (0,) * len(shape)
    if single_buffer:
        return pl.BlockSpec(shape, lambda *_: zeros, pipeline_mode=pl.Buffered(1))
    return pl.BlockSpec(shape, lambda *_: zeros)


def _rms_norm(x, gain):
    ms = jnp.mean(x * x, axis=-1, keepdims=True)
    return x * lax.rsqrt(ms + NORM_EPS) * gain


def _silu(x):
    return x * jax.nn.sigmoid(x)


def _even_in_kernel(x_ref, g_ref, w_ref, cos_ref, sin_ref, zeta_ref, *rest):
    n_cast = (len(rest) - 7) // 2
    cast_in = rest[:n_cast]
    u_ref, az_ref, q_ref, k_ref, kz_ref, v_ref, bz_ref = rest[n_cast:n_cast + 7]
    cast_out = rest[n_cast + 7:]
    for src, dst in zip(cast_in, cast_out):
        dst[...] = src[...].astype(BF16)

    hn = _rms_norm(x_ref[...], g_ref[...]).astype(BF16)

    def proj(j):
        return jnp.dot(hn, w_ref[:, j * D_MODEL:(j + 1) * D_MODEL], preferred_element_type=F32)

    def rotary_store(p, out_ref, scale, decayed_ref=None):
        cos = cos_ref[...]
        sin = sin_ref[...]
        half = RET_DK // 2
        for h in range(RET_HEADS):
            lo = slice(h * RET_DK, h * RET_DK + half)
            hi = slice(h * RET_DK + half, (h + 1) * RET_DK)
            x1 = p[:, lo]
            x2 = p[:, hi]
            o1 = x1 * cos - x2 * sin
            o2 = x1 * sin + x2 * cos
            if scale != 1.0:
                o1 = o1 * scale
                o2 = o2 * scale
            out_ref[:, lo] = o1.astype(BF16)
            out_ref[:, hi] = o2.astype(BF16)
            if decayed_ref is not None:
                decayed_ref[:, lo] = (o1 * zeta_ref[:, lo]).astype(BF16)
                decayed_ref[:, hi] = (o2 * zeta_ref[:, hi]).astype(BF16)

    u = proj(0).astype(BF16)
    for s in range(S5_SLABS):
        u_ref[s] = u[:, s * LANES:(s + 1) * LANES]
    az_ref[...] = _silu(proj(1)).astype(BF16)
    rotary_store(proj(2), q_ref, 1.0)
    rotary_store(proj(3), k_ref, RET_DK ** -0.5, kz_ref)
    v_ref[...] = proj(4).astype(BF16)
    bz_ref[...] = _silu(proj(5)).astype(BF16)


def _even_in_proj(x2d, gain, w_in, cos, sin, zeta, seq, later_weights):
    n = x2d.shape[0]
    tm = PROJ_TILE
    steps = n // tm
    tiles_per_seq = seq // tm
    tok = pl.BlockSpec((tm, D_MODEL), lambda i: (i, 0))
    rot = pl.BlockSpec((tm, RET_DK // 2), lambda i: (i % tiles_per_seq, 0))
    out = jax.ShapeDtypeStruct((n, D_MODEL), BF16)
    cast_specs = [pl.BlockSpec((w.shape[0] // steps, w.shape[1]), lambda i: (i, 0))
                  for w in later_weights]
    res = pl.pallas_call(
        _even_in_kernel,
        grid=(steps,),
        in_specs=[tok, _const_spec((1, D_MODEL)), _const_spec(w_in.shape), rot, rot,
                  _const_spec(zeta.shape)] + cast_specs,
        out_specs=([pl.BlockSpec((S5_SLABS, tm, LANES), lambda i: (0, i, 0))] + [tok] * 6
                   + cast_specs),
        out_shape=([jax.ShapeDtypeStruct((S5_SLABS, n, LANES), BF16)] + [out] * 6
                   + [jax.ShapeDtypeStruct(w.shape, BF16) for w in later_weights]),
        compiler_params=pltpu.CompilerParams(
            dimension_semantics=("arbitrary",), vmem_limit_bytes=VMEM_LIMIT_BYTES),
        name="even_in_proj",
    )(x2d, gain, w_in, cos, sin, zeta, *later_weights)
    return res[:7], res[7:]


def _s5_kernel(u_ref, kw_ref, cp_ref, bpg_ref, pcon_ref, econ_ref, a_ref, y_ref,
               tokbuf, zt, ytbuf, vbuf, spbuf, ebuf, sinbuf, wyt, bp, *, n_blk, seg_len):
    lanes = S5_CHUNK_LANES
    re = slice(0, lanes)
    im = slice(lanes, 2 * lanes)
    pairs = S5_CHUNK_PAIRS
    blk_w = S5_BLOCK * S5_GROUP

    @pl.when(pl.program_id(1) == 0)
    def _():
        kw = kw_ref[...].reshape(S5_CHUNK_GROUPS * S5_GROUP, blk_w)
        for jo in range(S5_BLOCK):
            rows = jnp.dot(kw, pcon_ref[jo], preferred_element_type=F32).astype(BF16)
            for gl in range(S5_CHUNK_GROUPS):
                wyt[gl, jo * S5_GROUP:(jo + 1) * S5_GROUP, 0:blk_w] = (
                    rows[gl * S5_GROUP:(gl + 1) * S5_GROUP, :])
        for gl in range(S5_CHUNK_GROUPS):
            wyt[gl, :, blk_w:] = cp_ref[gl]
        for q in range(pairs):
            for par in range(2):
                bp[q, par * blk_w:(par + 1) * blk_w, :] = jnp.dot(
                    bpg_ref[2 * q + par], econ_ref[par], preferred_element_type=F32).astype(BF16)

    for b in range(n_blk):
        tokbuf[b * S5_TOK_PITCH:b * S5_TOK_PITCH + S5_BLOCK, :] = (
            u_ref[b * S5_BLOCK:(b + 1) * S5_BLOCK, :].astype(F32))
    for j in range(S5_BLOCK):
        ujt = tokbuf[pl.ds(j, n_blk, stride=S5_TOK_PITCH), :].astype(BF16).T
        for gl in range(S5_CHUNK_GROUPS):
            zt[gl, j * S5_GROUP:(j + 1) * S5_GROUP, :] = ujt[gl * S5_GROUP:(gl + 1) * S5_GROUP, :]

    for q in range(pairs):
        zc = jnp.concatenate([zt[2 * q], zt[2 * q + 1]], axis=0)
        v = lax.dot_general(zc, bp[q], (((0,), (0,)), ((), ())),
                            preferred_element_type=F32)
        for seg in range(S5_SEGS):
            src = slice(seg * seg_len, (seg + 1) * seg_len)
            dst = slice(seg * S5_SEG_PITCH, seg * S5_SEG_PITCH + seg_len)
            vbuf[q, dst, :] = v[src, 0:LANES]
            vbuf[pairs + q, dst, :] = v[src, LANES:]

    ar = jnp.broadcast_to(a_ref[0:1, :], (SUBLANES, lanes))
    ai = jnp.broadcast_to(a_ref[1:2, :], (SUBLANES, lanes))

    def advance(sr, si, i):
        rows = pl.ds(i, S5_SEGS, stride=S5_SEG_PITCH)
        xr = jnp.concatenate([vbuf[s, rows, :] for s in range(pairs)], axis=1)
        xi = jnp.concatenate([vbuf[pairs + s, rows, :] for s in range(pairs)], axis=1)
        return ar * sr - ai * si + xr, ar * si + ai * sr + xi

    def end_state(i, state):
        return advance(*state, i)

    zero = jnp.zeros((SUBLANES, lanes), F32)
    er, ei = lax.fori_loop(0, seg_len, end_state, (zero, zero), unroll=8)
    ebuf[:, re] = er
    ebuf[:, im] = ei

    atr = a_ref[2:3, :]
    ati = a_ref[3:4, :]
    cr = jnp.zeros((1, lanes), F32)
    ci = jnp.zeros((1, lanes), F32)
    for seg in range(S5_SEGS):
        sinbuf[seg:seg + 1, re] = cr
        sinbuf[seg:seg + 1, im] = ci
        cr, ci = (atr * cr - ati * ci + ebuf[seg:seg + 1, re],
                  atr * ci + ati * cr + ebuf[seg:seg + 1, im])

    def record(i, state):
        sr, si = state
        rows = pl.ds(i, S5_SEGS, stride=S5_SEG_PITCH)
        for s in range(pairs):
            spbuf[s, rows, :] = sr[:, s * LANES:(s + 1) * LANES]
            spbuf[pairs + s, rows, :] = si[:, s * LANES:(s + 1) * LANES]
        return advance(sr, si, i)

    lax.fori_loop(0, seg_len, record, (sinbuf[:, re], sinbuf[:, im]), unroll=8)

    def natural_rows(slab):
        return jnp.concatenate(
            [spbuf[slab, seg * S5_SEG_PITCH:seg * S5_SEG_PITCH + seg_len, :]
             for seg in range(S5_SEGS)], axis=0)

    for q in range(pairs):
        sp = jnp.concatenate([natural_rows(q), natural_rows(pairs + q)], axis=1)
        spt = sp.astype(BF16).T
        for par in range(2):
            gl = 2 * q + par
            rhs = jnp.concatenate(
                [zt[gl], spt[par * S5_STATE:(par + 1) * S5_STATE],
                 spt[LANES + par * S5_STATE:LANES + (par + 1) * S5_STATE]], axis=0)
            ytbuf[gl] = jnp.dot(wyt[gl], rhs, preferred_element_type=F32)

    for j in range(S5_BLOCK):
        yt = jnp.concatenate([ytbuf[gl, j * S5_GROUP:(j + 1) * S5_GROUP, :]
                              for gl in range(S5_CHUNK_GROUPS)], axis=0)
        tokbuf[pl.ds(j, n_blk, stride=S5_TOK_PITCH), :] = yt.astype(BF16).T.astype(F32)
    for b in range(n_blk):
        y_ref[b * S5_BLOCK:(b + 1) * S5_BLOCK, :] = (
            tokbuf[b * S5_TOK_PITCH:b * S5_TOK_PITCH + S5_BLOCK, :].astype(BF16))


def _s5_params(lam_re, lam_im, log_dt, b_re, b_im, c_re, c_im, seg_len):
    hp = lax.Precision.HIGHEST
    r = S5_BLOCK
    g_n, p_n, h_n = S5_GROUPS, S5_STATE, S5_GROUP
    lr = jnp.minimum(lam_re.astype(F32), -1e-4)
    li = lam_im.astype(F32)
    dt = jnp.exp(log_dt.astype(F32))[:, None]
    mag = jnp.exp(lr * dt)
    ab_re = mag * jnp.cos(li * dt)
    ab_im = mag * jnp.sin(li * dt)
    den = lr * lr + li * li
    n_re = ab_re - 1.0
    n_im = ab_im
    z_re = (n_re * lr + n_im * li) / den
    z_im = (n_im * lr - n_re * li) / den
    br = b_re.astype(F32)
    bi = b_im.astype(F32)
    bb_re = z_re[..., None] * br - z_im[..., None] * bi
    bb_im = z_re[..., None] * bi + z_im[..., None] * br

    ldt = (lr * dt)[:, None, :]
    wdt = (li * dt)[:, None, :]

    def a_pow(k):
        pm = jnp.exp(ldt * k)
        return pm * jnp.cos(wdt * k), pm * jnp.sin(wdt * k)

    pr, pi = a_pow(jnp.arange(r + 1, dtype=F32)[None, :, None])
    cr = c_re.astype(F32)
    ci = c_im.astype(F32)

    car_re = cr[:, :, None, :] * pr[:, None, :r] - ci[:, :, None, :] * pi[:, None, :r]
    car_im = cr[:, :, None, :] * pi[:, None, :r] + ci[:, :, None, :] * pr[:, None, :r]
    kt = (jnp.einsum('galp,gph->galh', car_re, bb_re, precision=hp)
          - jnp.einsum('galp,gph->galh', car_im, bb_im, precision=hp))
    kw = kt.reshape(g_n, h_n, r * h_n).astype(BF16)

    cre = (cr[:, None] * pr[:, 1:, None, :] - ci[:, None] * pi[:, 1:, None, :]).reshape(
        g_n, r * h_n, p_n)
    cim = -(cr[:, None] * pi[:, 1:, None, :] + ci[:, None] * pr[:, 1:, None, :]).reshape(
        g_n, r * h_n, p_n)
    cp = jnp.concatenate([cre, cim], axis=2).astype(BF16)

    bbr = jnp.transpose(bb_re, (0, 2, 1))[:, None]
    bbi = jnp.transpose(bb_im, (0, 2, 1))[:, None]
    prr = pr[:, r - 1::-1][:, :, None, :]
    pir = pi[:, r - 1::-1][:, :, None, :]
    bpg = jnp.concatenate([(prr * bbr - pir * bbi).reshape(g_n, r * h_n, p_n),
                           (prr * bbi + pir * bbr).reshape(g_n, r * h_n, p_n)],
                          axis=2).astype(BF16)

    at_re, at_im = a_pow(float(r * seg_len))
    n_chunks = g_n // S5_CHUNK_GROUPS
    a_tab = jnp.stack([x.reshape(n_chunks, S5_CHUNK_LANES)
                       for x in (pr[:, r], pi[:, r], at_re, at_im)], axis=1)
    return kw, cp, bpg, a_tab


def _s5_placement_tables():
    r, h_n, p_n = S5_BLOCK, S5_GROUP, S5_STATE
    pcon = np.zeros((r, r, h_n, r, h_n), np.float32)
    for jo in range(r):
        for ji in range(jo + 1):
            pcon[jo, jo - ji, np.arange(h_n), ji, np.arange(h_n)] = 1.0
    econ = np.zeros((2, 2 * p_n, 4 * p_n), np.float32)
    for par in range(2):
        econ[par, np.arange(p_n), par * p_n + np.arange(p_n)] = 1.0
        econ[par, p_n + np.arange(p_n), (2 + par) * p_n + np.arange(p_n)] = 1.0
    return (jnp.asarray(pcon.reshape(r, r * h_n, r * h_n), BF16), jnp.asarray(econ, BF16))


def _s5_branch(u, kw, cp, bpg, a_tab, bsz, seq):
    n_blk = seq // S5_BLOCK
    seg_len = n_blk // S5_SEGS
    width = S5_BLOCK * S5_GROUP
    slab = S5_CHUNK_GROUPS * S5_GROUP
    scan_rows = S5_SEGS * S5_SEG_PITCH
    assert slab == LANES and S5_GROUPS // S5_CHUNK_GROUPS == S5_SLABS
    pcon, econ = _s5_placement_tables()
    tok = pl.BlockSpec((None, seq, slab), lambda c, b: (c, b, 0))

    def per_chunk(shape):
        return pl.BlockSpec((S5_CHUNK_GROUPS,) + shape, lambda c, b: (c, 0, 0))

    return pl.pallas_call(
        functools.partial(_s5_kernel, n_blk=n_blk, seg_len=seg_len),
        grid=(S5_GROUPS // S5_CHUNK_GROUPS, bsz),
        in_specs=[tok, per_chunk((S5_GROUP, width)), per_chunk((width, 2 * S5_STATE)),
                  per_chunk((width, 2 * S5_STATE)), _const_spec(pcon.shape), _const_spec(econ.shape),
                  pl.BlockSpec((None, 4, S5_CHUNK_LANES), lambda c, b: (c, 0, 0))],
        out_specs=tok,
        out_shape=jax.ShapeDtypeStruct(u.shape, BF16),
        scratch_shapes=[
            pltpu.VMEM((n_blk * S5_TOK_PITCH, slab), F32),
            pltpu.VMEM((S5_CHUNK_GROUPS, width, n_blk), BF16),
            pltpu.VMEM((S5_CHUNK_GROUPS, width, n_blk), F32),
            pltpu.VMEM((2 * S5_CHUNK_PAIRS, scan_rows, LANES), F32),
            pltpu.VMEM((2 * S5_CHUNK_PAIRS, scan_rows, LANES), F32),
            pltpu.VMEM((S5_SEGS, 2 * S5_CHUNK_LANES), F32),
            pltpu.VMEM((S5_SEGS, 2 * S5_CHUNK_LANES), F32),
            pltpu.VMEM((S5_CHUNK_GROUPS, width, width + 2 * S5_STATE), BF16),
            pltpu.VMEM((S5_CHUNK_PAIRS, 2 * width, 4 * S5_STATE), BF16),
        ],
        compiler_params=pltpu.CompilerParams(
            dimension_semantics=("arbitrary", "arbitrary"), vmem_limit_bytes=VMEM_LIMIT_BYTES),
        name="s5_branch",
    )(u, kw, cp, bpg, pcon, econ, a_tab)


def _retention_unit(ck, h, q_ref, k_ref, kz_ref, v_ref, bz_ref, gain_ref, decay_ref, xi_ref,
                    state, yb_ref, chunk_decay):
    rows = slice(ck * RET_CHUNK, (ck + 1) * RET_CHUNK)
    cols = slice(h * RET_DK, (h + 1) * RET_DK)
    qh = q_ref[rows, cols]
    vh = v_ref[rows, cols]
    scores = lax.dot_general(qh, k_ref[rows, cols], (((1,), (1,)), ((), ())),
                             preferred_element_type=F32)
    inner = jnp.dot((scores * decay_ref[h]).astype(BF16), vh, preferred_element_type=F32)
    prev = state[h]
    cross = jnp.dot(qh, prev.astype(BF16), preferred_element_type=F32) * xi_ref[:, cols]
    local = lax.dot_general(kz_ref[rows, cols], vh, (((0,), (0,)), ((), ())),
                            preferred_element_type=F32)
    state[h] = prev * chunk_decay[h] + local
    o = inner + cross
    mu = jnp.mean(o, axis=-1, keepdims=True)
    oc = o - mu
    var = jnp.mean(oc * oc, axis=-1, keepdims=True)
    o = oc * lax.rsqrt(var + NORM_EPS) * gain_ref[:, cols]
    yb_ref[rows, cols] = (o * bz_ref[rows, cols].astype(F32)).astype(BF16)


def _retention_tables():
    log_g = np.log1p(-np.exp2(-5.0 - np.arange(RET_HEADS, dtype=np.float64)))
    idx = np.arange(RET_CHUNK, dtype=np.float64)
    diff = idx[:, None] - idx[None, :]
    decay = np.where(diff >= 0, np.exp(log_g[:, None, None] * np.maximum(diff, 0.0)), 0.0)
    xi = np.repeat(np.exp(log_g[None, :] * (idx[:, None] + 1.0)), RET_DK, axis=1)
    zeta = np.repeat(np.exp(log_g[None, :] * (RET_CHUNK - 1.0 - idx[:, None])), RET_DK, axis=1)
    zeta = np.tile(zeta, (PROJ_TILE // RET_CHUNK, 1))
    chunk_decay = tuple(float(x) for x in np.exp(log_g * RET_CHUNK))
    return (jnp.asarray(decay, F32), jnp.asarray(xi, F32), jnp.asarray(zeta, F32), chunk_decay)


def _even_out_kernel(x_ref, ys_ref, u_ref, az_ref, q_ref, k_ref, kz_ref, v_ref, bz_ref,
                     d_ref, wglu_ref, bglu_ref, w_ref, gain_ref, decay_ref, xi_ref,
                     o_ref, state, ybuf, *, chunk_decay, tiles_per_seq):
    @pl.when(pl.program_id(0) % tiles_per_seq == 0)
    def _():
        state[...] = jnp.zeros_like(state)

    units = [(ck, h) for ck in range(OUT_TILE // RET_CHUNK) for h in range(RET_HEADS)]

    def retention(selected):
        for ck, h in selected:
            _retention_unit(ck, h, q_ref, k_ref, kz_ref, v_ref, bz_ref, gain_ref, decay_ref,
                            xi_ref, state, ybuf, chunk_decay)

    ys = jnp.concatenate([ys_ref[s] for s in range(S5_SLABS)], axis=1)
    u = jnp.concatenate([u_ref[s] for s in range(S5_SLABS)], axis=1)
    y = ys.astype(F32) + d_ref[...] * u.astype(F32)
    y = jax.nn.gelu(y)
    gl = jnp.dot(y.astype(BF16), wglu_ref[...], preferred_element_type=F32) + bglu_ref[...]
    n_chunks = OUT_TILE // RET_CHUNK
    retention(units[:RET_HEADS])
    ya = (y * jax.nn.sigmoid(gl) * az_ref[...].astype(F32)).astype(BF16)
    acc = jnp.dot(ya, w_ref[0:S5_WIDTH, :], preferred_element_type=F32)
    for ck in range(n_chunks):
        rows = slice(ck * RET_CHUNK, (ck + 1) * RET_CHUNK)
        part = jnp.dot(ybuf[rows, :], w_ref[S5_WIDTH:, :], preferred_element_type=F32)
        if ck + 1 < n_chunks:
            retention(units[(ck + 1) * RET_HEADS:(ck + 2) * RET_HEADS])
        o_ref[rows, :] = x_ref[rows, :] + (acc[rows, :] + part)


def _even_out_proj(x2d, ys, u, az, q, k, kz, v, bz, d_skip, w_glu, b_glu, w_out, gain,
                   decay, xi, chunk_decay, seq):
    n = x2d.shape[0]
    tm = OUT_TILE
    assert tm % RET_CHUNK == 0 and seq % tm == 0
    tok = pl.BlockSpec((tm, D_MODEL), lambda i: (i, 0))
    slabs = pl.BlockSpec((S5_SLABS, tm, LANES), lambda i: (0, i, 0))
    return pl.pallas_call(
        functools.partial(_even_out_kernel, chunk_decay=chunk_decay, tiles_per_seq=seq // tm),
        grid=(n // tm,),
        in_specs=[tok, slabs, slabs, tok, tok, tok, tok, tok, tok,
                  _const_spec((1, S5_WIDTH)), _const_spec(w_glu.shape), _const_spec((1, S5_WIDTH)),
                  _const_spec(w_out.shape), _const_spec((1, D_MODEL)), _const_spec(decay.shape),
                  _const_spec(xi.shape)],
        out_specs=tok,
        out_shape=jax.ShapeDtypeStruct(x2d.shape, F32),
        scratch_shapes=[pltpu.VMEM((RET_HEADS, RET_DK, RET_DV), F32),
                        pltpu.VMEM((tm, D_MODEL), BF16)],
        compiler_params=pltpu.CompilerParams(
            dimension_semantics=("arbitrary",), vmem_limit_bytes=VMEM_LIMIT_BYTES),
        name="even_out_proj",
    )(x2d, ys, u, az, q, k, kz, v, bz, d_skip, w_glu, b_glu, w_out, gain, decay, xi)


def _odd_kernel(x_ref, g_ref, w_ref, vgain_ref, wsp_ref, bsp_ref, wout_ref, fg_ref, o_ref,
                vbuf, ubuf, gbuf, vnbuf, ybuf):
    x = x_ref[...]
    hn = _rms_norm(x, g_ref[...]).astype(BF16)
    gd = SGU_GROUP_DIM

    def proj(col):
        return jnp.dot(hn, w_ref[:, col:col + gd], preferred_element_type=F32)

    for g in range(SGU_GROUPS):
        vbuf[:, g * gd:(g + 1) * gd] = jax.nn.gelu(proj(SGU_WIDTH + g * gd))
    for g in range(SGU_GROUPS):
        ubuf[:, g * gd:(g + 1) * gd] = jax.nn.gelu(proj(g * gd))
        gbuf[:, g * gd:(g + 1) * gd] = _silu(proj(2 * SGU_WIDTH + g * gd))
    v = vbuf[...]
    mu = jnp.mean(v, axis=-1, keepdims=True)
    vc = v - mu
    var = jnp.mean(vc * vc, axis=-1, keepdims=True)
    vnbuf[...] = (vc * lax.rsqrt(var + NORM_EPS) * vgain_ref[...]).astype(BF16)

    row = lax.broadcasted_iota(jnp.int32, (SGU_CHUNK, SGU_CHUNK), 0)
    col = lax.broadcasted_iota(jnp.int32, (SGU_CHUNK, SGU_CHUNK), 1)
    for g in range(SGU_GROUPS):
        cols = slice(g * gd, (g + 1) * gd)
        wm = jnp.where(row >= col, wsp_ref[g], 0.0).astype(BF16)
        bias = bsp_ref[g]
        for c in range(ODD_TILE // SGU_CHUNK):
            rows = slice(c * SGU_CHUNK, (c + 1) * SGU_CHUNK)
            s = jnp.dot(wm, vnbuf[rows, cols], preferred_element_type=F32) + bias
            ybuf[rows, cols] = (ubuf[rows, cols] * s * gbuf[rows, cols]).astype(BF16)

    xn = x + jnp.dot(ybuf[...], wout_ref[...], preferred_element_type=F32)
    o_ref[...] = _rms_norm(xn, fg_ref[...])


def _odd_layer(x2d, gain, w_in, vgain, wsp, bsp, w_out, final_gain):
    n = x2d.shape[0]
    tm = ODD_TILE
    tok = pl.BlockSpec((tm, D_MODEL), lambda i: (i, 0))
    return pl.pallas_call(
        _odd_kernel,
        grid=(n // tm,),
        in_specs=[tok, _const_spec((1, D_MODEL)), _const_spec(w_in.shape),
                  _const_spec((1, SGU_WIDTH)), _const_spec(wsp.shape), _const_spec(bsp.shape),
                  _const_spec(w_out.shape), _const_spec((1, D_MODEL))],
        out_specs=tok,
        out_shape=jax.ShapeDtypeStruct(x2d.shape, F32),
        scratch_shapes=[pltpu.VMEM((tm, SGU_WIDTH), F32),
                        pltpu.VMEM((tm, SGU_WIDTH), F32),
                        pltpu.VMEM((tm, SGU_WIDTH), F32),
                        pltpu.VMEM((tm, SGU_WIDTH), BF16),
                        pltpu.VMEM((tm, SGU_WIDTH), BF16)],
        compiler_params=pltpu.CompilerParams(
            dimension_semantics=("arbitrary",), vmem_limit_bytes=VMEM_LIMIT_BYTES),
        name="odd_layer",
    )(x2d, gain, w_in, vgain, wsp, bsp, w_out, final_gain)


def _rotary_tables(seq):
    half = RET_DK // 2
    pos = np.arange(seq, dtype=np.float64)
    inv = ROPE_BASE ** (-np.arange(half, dtype=np.float64) / half)
    ang = pos[:, None] * inv[None, :]
    return jnp.asarray(np.cos(ang), F32), jnp.asarray(np.sin(ang), F32)


def kernel(x, norm_even, w_in_even, s5_lam_re, s5_lam_im, s5_log_dt, s5_b_re, s5_b_im, s5_c_re, s5_c_im, s5_d, s5_w_glu, s5_b_glu, ret_gn_gain, w_out_even, norm_odd, w_in_odd, sgu_norm_gain, sgu_w_spatial, sgu_b_spatial, w_out_odd, final_norm):
    bsz, seq, d = x.shape
    x2d = x.reshape(bsz * seq, d)
    cos, sin = _rotary_tables(seq)

    decay, xi, zeta, chunk_decay = _retention_tables()
    (u, az, q, k, kz, v, bz), (w_glu, w_out_e, w_in_o, w_out_o) = _even_in_proj(
        x2d, norm_even[0].reshape(1, d), w_in_even[0].astype(BF16), cos, sin, zeta, seq,
        [s5_w_glu[0], w_out_even[0], w_in_odd[0], w_out_odd[0]])
    seg_len = seq // S5_BLOCK // S5_SEGS
    kw, cp, bpg, a_tab = _s5_params(s5_lam_re[0], s5_lam_im[0], s5_log_dt[0], s5_b_re[0],
                                    s5_b_im[0], s5_c_re[0], s5_c_im[0], seg_len)
    ys = _s5_branch(u, kw, cp, bpg, a_tab, bsz, seq)
    x2d = _even_out_proj(x2d, ys, u, az, q, k, kz, v, bz, s5_d[0].reshape(1, -1).astype(F32),
                         w_glu, s5_b_glu[0].reshape(1, -1).astype(F32), w_out_e,
                         ret_gn_gain[0].reshape(1, -1).astype(F32), decay, xi, chunk_decay, seq)

    out = _odd_layer(x2d, norm_odd[0].reshape(1, d), w_in_o,
                     sgu_norm_gain[0].reshape(1, -1).astype(F32), sgu_w_spatial[0].astype(F32),
                     sgu_b_spatial[0].astype(F32)[:, :, None], w_out_o,
                     final_norm.reshape(1, d))
    return out.reshape(bsz, seq, d)
```

```python
import functools

import jax
import jax.numpy as jnp
import numpy as np
from jax import lax
from jax.experimental import pallas as pl
from jax.experimental.pallas import tpu as pltpu

F32 = jnp.float32
BF16 = jnp.bfloat16

D_MODEL = 1024
S5_WIDTH = 1024
S5_GROUP = 16
S5_GROUPS = 64
S5_STATE = 64
RET_HEADS = 4
RET_DK = 256
RET_DV = 256
ROPE_BASE = 10000.0
SGU_WIDTH = 2048
SGU_GROUPS = 4
SGU_GROUP_DIM = 512
SGU_CHUNK = 128
NORM_EPS = 1e-6

V7X_VMEM_BYTES = 64 * 1024 * 1024
VMEM_LIMIT_BYTES = V7X_VMEM_BYTES - 12 * 1024 * 1024
SUBLANES = 8
LANES = 128

PROJ_TILE = 512
OUT_TILE = 512
S5_BLOCK = 16
S5_SEGS = SUBLANES
S5_CHUNK_GROUPS = 8
S5_CHUNK_PAIRS = S5_CHUNK_GROUPS // 2
S5_CHUNK_LANES = S5_CHUNK_GROUPS * S5_STATE
S5_SLABS = S5_WIDTH // LANES
S5_SEG_PITCH = 72
S5_TOK_PITCH = 24
RET_CHUNK = 256
ODD_TILE = 512


def _const_spec(shape, single_buffer=True):
    zeros = (0,) * len(shape)
    if single_buffer:
        return pl.BlockSpec(shape, lambda *_: zeros, pipeline_mode=pl.Buffered(1))
    return pl.BlockSpec(shape, lambda *_: zeros)


def _rms_norm(x, gain):
    ms = jnp.mean(x * x, axis=-1, keepdims=True)
    return x * lax.rsqrt(ms + NORM_EPS) * gain


def _silu(x):
    return x * jax.nn.sigmoid(x)


def _even_in_kernel(x_ref, xnext_ref, g_ref, w_ref, cos_ref, sin_ref, zeta_ref, *rest):
    n_cast = (len(rest) - 8) // 2
    cast_in = rest[:n_cast]
    u_ref, az_ref, q_ref, k_ref, kz_ref, v_ref, bz_ref = rest[n_cast:n_cast + 7]
    cast_out = rest[n_cast + 7:2 * n_cast + 7]
    hn_scr = rest[-1]
    i = pl.program_id(0)
    slot = i % 2

    @pl.when(i == 0)
    def _():
        hn_scr[0] = _rms_norm(x_ref[...], g_ref[...]).astype(BF16)

    def proj(j):
        return jnp.dot(hn_scr[slot], w_ref[:, j * D_MODEL:(j + 1) * D_MODEL],
                       preferred_element_type=F32)

    def rotary_store(p, out_ref, scale, decayed_ref=None):
        cos = cos_ref[...]
        sin = sin_ref[...]
        half = RET_DK // 2
        for h in range(RET_HEADS):
            lo = slice(h * RET_DK, h * RET_DK + half)
            hi = slice(h * RET_DK + half, (h + 1) * RET_DK)
            x1 = p[:, lo]
            x2 = p[:, hi]
            o1 = x1 * cos - x2 * sin
            o2 = x1 * sin + x2 * cos
            if scale != 1.0:
                o1 = o1 * scale
                o2 = o2 * scale
            out_ref[:, lo] = o1.astype(BF16)
            out_ref[:, hi] = o2.astype(BF16)
            if decayed_ref is not None:
                decayed_ref[:, lo] = (o1 * zeta_ref[:, lo]).astype(BF16)
                decayed_ref[:, hi] = (o2 * zeta_ref[:, hi]).astype(BF16)

    u = proj(0).astype(BF16)
    for s in range(S5_SLABS):
        u_ref[s] = u[:, s * LANES:(s + 1) * LANES]
    az_ref[...] = _silu(proj(1)).astype(BF16)
    rotary_store(proj(2), q_ref, 1.0)
    rotary_store(proj(3), k_ref, RET_DK ** -0.5, kz_ref)
    v_ref[...] = proj(4).astype(BF16)
    hn_scr[1 - slot] = _rms_norm(xnext_ref[...], g_ref[...]).astype(BF16)
    for src, dst in zip(cast_in, cast_out):
        dst[...] = src[...].astype(BF16)
    bz_ref[...] = _silu(proj(5)).astype(BF16)


def _even_in_proj(x2d, gain, w_in, cos, sin, zeta, seq, later_weights):
    n = x2d.shape[0]
    tm = PROJ_TILE
    steps = n // tm
    tiles_per_seq = seq // tm
    tok = pl.BlockSpec((tm, D_MODEL), lambda i: (i, 0))
    tok_next = pl.BlockSpec((tm, D_MODEL), lambda i: (jnp.minimum(i + 1, steps - 1), 0))
    rot = pl.BlockSpec((tm, RET_DK // 2), lambda i: (i % tiles_per_seq, 0))
    out = jax.ShapeDtypeStruct((n, D_MODEL), BF16)
    cast_specs = [pl.BlockSpec((w.shape[0] // steps, w.shape[1]), lambda i: (i, 0))
                  for w in later_weights]
    res = pl.pallas_call(
        _even_in_kernel,
        grid=(steps,),
        in_specs=[tok, tok_next, _const_spec((1, D_MODEL)), _const_spec(w_in.shape), rot, rot,
                  _const_spec(zeta.shape)] + cast_specs,
        out_specs=([pl.BlockSpec((S5_SLABS, tm, LANES), lambda i: (0, i, 0))] + [tok] * 6
                   + cast_specs),
        out_shape=([jax.ShapeDtypeStruct((S5_SLABS, n, LANES), BF16)] + [out] * 6
                   + [jax.ShapeDtypeStruct(w.shape, BF16) for w in later_weights]),
        scratch_shapes=[pltpu.VMEM((2, tm, D_MODEL), BF16)],
        compiler_params=pltpu.CompilerParams(
            dimension_semantics=("arbitrary",), vmem_limit_bytes=VMEM_LIMIT_BYTES),
        name="even_in_proj",
    )(x2d, x2d, gain, w_in, cos, sin, zeta, *later_weights)
    return res[:7], res[7:]


def _s5_kernel(u_ref, kw_ref, cp_ref, bpg_ref, pcon_ref, econ_ref, a_ref, y_ref,
               tokbuf, zt, ytbuf, vbuf, spbuf, ebuf, sinbuf, wyt, bp, *, n_blk, seg_len):
    lanes = S5_CHUNK_LANES
    re = slice(0, lanes)
    im = slice(lanes, 2 * lanes)
    pairs = S5_CHUNK_PAIRS
    blk_w = S5_BLOCK * S5_GROUP

    @pl.when(pl.program_id(1) == 0)
    def _():
        kw = kw_ref[...].reshape(S5_CHUNK_GROUPS * S5_GROUP, blk_w)
        for jo in range(S5_BLOCK):
            rows = jnp.dot(kw, pcon_ref[jo], preferred_element_type=F32).astype(BF16)
            for gl in range(S5_CHUNK_GROUPS):
                wyt[gl, jo * S5_GROUP:(jo + 1) * S5_GROUP, 0:blk_w] = (
                    rows[gl * S5_GROUP:(gl + 1) * S5_GROUP, :])
        for gl in range(S5_CHUNK_GROUPS):
            wyt[gl, :, blk_w:] = cp_ref[gl]
        for q in range(pairs):
            for par in range(2):
                bp[q, par * blk_w:(par + 1) * blk_w, :] = jnp.dot(
                    bpg_ref[2 * q + par], econ_ref[par], preferred_element_type=F32).astype(BF16)

    for b in range(n_blk):
        tokbuf[b * S5_TOK_PITCH:b * S5_TOK_PITCH + S5_BLOCK, :] = (
            u_ref[b * S5_BLOCK:(b + 1) * S5_BLOCK, :].astype(F32))
    for j in range(S5_BLOCK):
        ujt = tokbuf[pl.ds(j, n_blk, stride=S5_TOK_PITCH), :].astype(BF16).T
        for gl in range(S5_CHUNK_GROUPS):
            zt[gl, j * S5_GROUP:(j + 1) * S5_GROUP, :] = ujt[gl * S5_GROUP:(gl + 1) * S5_GROUP, :]

    for q in range(pairs):
        zc = jnp.concatenate([zt[2 * q], zt[2 * q + 1]], axis=0)
        v = lax.dot_general(zc, bp[q], (((0,), (0,)), ((), ())),
                            preferred_element_type=F32)
        for seg in range(S5_SEGS):
            src = slice(seg * seg_len, (seg + 1) * seg_len)
            dst = slice(seg * S5_SEG_PITCH, seg * S5_SEG_PITCH + seg_len)
            vbuf[q, dst, :] = v[src, 0:LANES]
            vbuf[pairs + q, dst, :] = v[src, LANES:]

    ar = jnp.broadcast_to(a_ref[0:1, :], (SUBLANES, lanes))
    ai = jnp.broadcast_to(a_ref[1:2, :], (SUBLANES, lanes))

    def advance(sr, si, i):
        rows = pl.ds(i, S5_SEGS, stride=S5_SEG_PITCH)
        xr = jnp.concatenate([vbuf[s, rows, :] for s in range(pairs)], axis=1)
        xi = jnp.concatenate([vbuf[pairs + s, rows, :] for s in range(pairs)], axis=1)
        return ar * sr - ai * si + xr, ar * si + ai * sr + xi

    def end_state(i, state):
        return advance(*state, i)

    zero = jnp.zeros((SUBLANES, lanes), F32)
    er, ei = lax.fori_loop(0, seg_len, end_state, (zero, zero), unroll=8)
    ebuf[:, re] = er
    ebuf[:, im] = ei

    atr = a_ref[2:3, :]
    ati = a_ref[3:4, :]
    cr = jnp.zeros((1, lanes), F32)
    ci = jnp.zeros((1, lanes), F32)
    for seg in range(S5_SEGS):
        sinbuf[seg:seg + 1, re] = cr
        sinbuf[seg:seg + 1, im] = ci
        cr, ci = (atr * cr - ati * ci + ebuf[seg:seg + 1, re],
                  atr * ci + ati * cr + ebuf[seg:seg + 1, im])

    def record(i, state):
        sr, si = state
        rows = pl.ds(i, S5_SEGS, stride=S5_SEG_PITCH)
        for s in range(pairs):
            spbuf[s, rows, :] = sr[:, s * LANES:(s + 1) * LANES]
            spbuf[pairs + s, rows, :] = si[:, s * LANES:(s + 1) * LANES]
        return advance(sr, si, i)

    lax.fori_loop(0, seg_len, record, (sinbuf[:, re], sinbuf[:, im]), unroll=8)

    def natural_rows(slab):
        return jnp.concatenate(
            [spbuf[slab, seg * S5_SEG_PITCH:seg * S5_SEG_PITCH + seg_len, :]
             for seg in range(S5_SEGS)], axis=0)

    for q in range(pairs):
        sp = jnp.concatenate([natural_rows(q), natural_rows(pairs + q)], axis=1)
        spt = sp.astype(BF16).T
        for par in range(2):
            gl = 2 * q + par
            rhs = jnp.concatenate(
                [zt[gl], spt[par * S5_STATE:(par + 1) * S5_STATE],
                 spt[LANES + par * S5_STATE:LANES + (par + 1) * S5_STATE]], axis=0)
            ytbuf[gl] = jnp.dot(wyt[gl], rhs, preferred_element_type=F32)

    for j in range(S5_BLOCK):
        yt = jnp.concatenate([ytbuf[gl, j * S5_GROUP:(j + 1) * S5_GROUP, :]
                              for gl in range(S5_CHUNK_GROUPS)], axis=0)
        tokbuf[pl.ds(j, n_blk, stride=S5_TOK_PITCH), :] = yt.astype(BF16).T.astype(F32)
    for b in range(n_blk):
        y_ref[b * S5_BLOCK:(b + 1) * S5_BLOCK, :] = (
            tokbuf[b * S5_TOK_PITCH:b * S5_TOK_PITCH + S5_BLOCK, :].astype(BF16))


def _s5_params(lam_re, lam_im, log_dt, b_re, b_im, c_re, c_im, seg_len):
    hp = lax.Precision.HIGHEST
    r = S5_BLOCK
    g_n, p_n, h_n = S5_GROUPS, S5_STATE, S5_GROUP
    lr = jnp.minimum(lam_re.astype(F32), -1e-4)
    li = lam_im.astype(F32)
    dt = jnp.exp(log_dt.astype(F32))[:, None]
    mag = jnp.exp(lr * dt)
    ab_re = mag * jnp.cos(li * dt)
    ab_im = mag * jnp.sin(li * dt)
    den = lr * lr + li * li
    n_re = ab_re - 1.0
    n_im = ab_im
    z_re = (n_re * lr + n_im * li) / den
    z_im = (n_im * lr - n_re * li) / den
    br = b_re.astype(F32)
    bi = b_im.astype(F32)
    bb_re = z_re[..., None] * br - z_im[..., None] * bi
    bb_im = z_re[..., None] * bi + z_im[..., None] * br

    ldt = (lr * dt)[:, None, :]
    wdt = (li * dt)[:, None, :]

    def a_pow(k):
        pm = jnp.exp(ldt * k)
        return pm * jnp.cos(wdt * k), pm * jnp.sin(wdt * k)

    pr, pi = a_pow(jnp.arange(r + 1, dtype=F32)[None, :, None])
    cr = c_re.astype(F32)
    ci = c_im.astype(F32)

    car_re = cr[:, :, None, :] * pr[:, None, :r] - ci[:, :, None, :] * pi[:, None, :r]
    car_im = cr[:, :, None, :] * pi[:, None, :r] + ci[:, :, None, :] * pr[:, None, :r]
    kt = (jnp.einsum('galp,gph->galh', car_re, bb_re, precision=hp)
          - jnp.einsum('galp,gph->galh', car_im, bb_im, precision=hp))
    kw = kt.reshape(g_n, h_n, r * h_n).astype(BF16)

    cre = (cr[:, None] * pr[:, 1:, None, :] - ci[:, None] * pi[:, 1:, None, :]).reshape(
        g_n, r * h_n, p_n)
    cim = -(cr[:, None] * pi[:, 1:, None, :] + ci[:, None] * pr[:, 1:, None, :]).reshape(
        g_n, r * h_n, p_n)
    cp = jnp.concatenate([cre, cim], axis=2).astype(BF16)

    bbr = jnp.transpose(bb_re, (0, 2, 1))[:, None]
    bbi = jnp.transpose(bb_im, (0, 2, 1))[:, None]
    prr = pr[:, r - 1::-1][:, :, None, :]
    pir = pi[:, r - 1::-1][:, :, None, :]
    bpg = jnp.concatenate([(prr * bbr - pir * bbi).reshape(g_n, r * h_n, p_n),
                           (prr * bbi + pir * bbr).reshape(g_n, r * h_n, p_n)],
                          axis=2).astype(BF16)

    at_re, at_im = a_pow(float(r * seg_len))
    n_chunks = g_n // S5_CHUNK_GROUPS
    a_tab = jnp.stack([x.reshape(n_chunks, S5_CHUNK_LANES)
                       for x in (pr[:, r], pi[:, r], at_re, at_im)], axis=1)
    return kw, cp, bpg, a_tab


def _s5_placement_tables():
    r, h_n, p_n = S5_BLOCK, S5_GROUP, S5_STATE
    pcon = np.zeros((r, r, h_n, r, h_n), np.float32)
    for jo in range(r):
        for ji in range(jo + 1):
            pcon[jo, jo - ji, np.arange(h_n), ji, np.arange(h_n)] = 1.0
    econ = np.zeros((2, 2 * p_n, 4 * p_n), np.float32)
    for par in range(2):
        econ[par, np.arange(p_n), par * p_n + np.arange(p_n)] = 1.0
        econ[par, p_n + np.arange(p_n), (2 + par) * p_n + np.arange(p_n)] = 1.0
    return (jnp.asarray(pcon.reshape(r, r * h_n, r * h_n), BF16), jnp.asarray(econ, BF16))


def _s5_branch(u, kw, cp, bpg, a_tab, bsz, seq):
    n_blk = seq // S5_BLOCK
    seg_len = n_blk // S5_SEGS
    width = S5_BLOCK * S5_GROUP
    slab = S5_CHUNK_GROUPS * S5_GROUP
    scan_rows = S5_SEGS * S5_SEG_PITCH
    assert slab == LANES and S5_GROUPS // S5_CHUNK_GROUPS == S5_SLABS
    pcon, econ = _s5_placement_tables()
    tok = pl.BlockSpec((None, seq, slab), lambda c, b: (c, b, 0))

    def per_chunk(shape):
        return pl.BlockSpec((S5_CHUNK_GROUPS,) + shape, lambda c, b: (c, 0, 0))

    return pl.pallas_call(
        functools.partial(_s5_kernel, n_blk=n_blk, seg_len=seg_len),
        grid=(S5_GROUPS // S5_CHUNK_GROUPS, bsz),
        in_specs=[tok, per_chunk((S5_GROUP, width)), per_chunk((width, 2 * S5_STATE)),
                  per_chunk((width, 2 * S5_STATE)), _const_spec(pcon.shape), _const_spec(econ.shape),
                  pl.BlockSpec((None, 4, S5_CHUNK_LANES), lambda c, b: (c, 0, 0))],
        out_specs=tok,
        out_shape=jax.ShapeDtypeStruct(u.shape, BF16),
        scratch_shapes=[
            pltpu.VMEM((n_blk * S5_TOK_PITCH, slab), F32),
            pltpu.VMEM((S5_CHUNK_GROUPS, width, n_blk), BF16),
            pltpu.VMEM((S5_CHUNK_GROUPS, width, n_blk), F32),
            pltpu.VMEM((2 * S5_CHUNK_PAIRS, scan_rows, LANES), F32),
            pltpu.VMEM((2 * S5_CHUNK_PAIRS, scan_rows, LANES), F32),
            pltpu.VMEM((S5_SEGS, 2 * S5_CHUNK_LANES), F32),
            pltpu.VMEM((S5_SEGS, 2 * S5_CHUNK_LANES), F32),
            pltpu.VMEM((S5_CHUNK_GROUPS, width, width + 2 * S5_STATE), BF16),
            pltpu.VMEM((S5_CHUNK_PAIRS, 2 * width, 4 * S5_STATE), BF16),
        ],
        compiler_params=pltpu.CompilerParams(
            dimension_semantics=("arbitrary", "arbitrary"), vmem_limit_bytes=VMEM_LIMIT_BYTES),
        name="s5_branch",
    )(u, kw, cp, bpg, pcon, econ, a_tab)


def _retention_unit(ck, h, q_ref, k_ref, kz_ref, v_ref, bz_ref, gain_ref, decay_ref, xi_ref,
                    state, yb_ref, chunk_decay):
    rows = slice(ck * RET_CHUNK, (ck + 1) * RET_CHUNK)
    cols = slice(h * RET_DK, (h + 1) * RET_DK)
    qh = q_ref[rows, cols]
    vh = v_ref[rows, cols]
    scores = lax.dot_general(qh, k_ref[rows, cols], (((1,), (1,)), ((), ())),
                             preferred_element_type=F32)
    inner = jnp.dot((scores * decay_ref[h]).astype(BF16), vh, preferred_element_type=F32)
    prev = state[h]
    cross = jnp.dot(qh, prev.astype(BF16), preferred_element_type=F32) * xi_ref[:, cols]
    local = lax.dot_general(kz_ref[rows, cols], vh, (((0,), (0,)), ((), ())),
                            preferred_element_type=F32)
    state[h] = prev * chunk_decay[h] + local
    o = inner + cross
    mu = jnp.mean(o, axis=-1, keepdims=True)
    oc = o - mu
    var = jnp.mean(oc * oc, axis=-1, keepdims=True)
    o = oc * lax.rsqrt(var + NORM_EPS) * gain_ref[:, cols]
    yb_ref[rows, cols] = (o * bz_ref[rows, cols].astype(F32)).astype(BF16)


def _retention_tables():
    log_g = np.log1p(-np.exp2(-5.0 - np.arange(RET_HEADS, dtype=np.float64)))
    idx = np.arange(RET_CHUNK, dtype=np.float64)
    diff = idx[:, None] - idx[None, :]
    decay = np.where(diff >= 0, np.exp(log_g[:, None, None] * np.maximum(diff, 0.0)), 0.0)
    xi = np.repeat(np.exp(log_g[None, :] * (idx[:, None] + 1.0)), RET_DK, axis=1)
    zeta = np.repeat(np.exp(log_g[None, :] * (RET_CHUNK - 1.0 - idx[:, None])), RET_DK, axis=1)
    zeta = np.tile(zeta, (PROJ_TILE // RET_CHUNK, 1))
    chunk_decay = tuple(float(x) for x in np.exp(log_g * RET_CHUNK))
    return (jnp.asarray(decay, F32), jnp.asarray(xi, F32), jnp.asarray(zeta, F32), chunk_decay)


def _even_out_kernel(x_ref, ys_ref, u_ref, az_ref, q_ref, k_ref, kz_ref, v_ref, bz_ref,
                     d_ref, wglu_ref, bglu_ref, w_ref, gain_ref, decay_ref, xi_ref,
                     o_ref, state, ybuf, *, chunk_decay, tiles_per_seq):
    @pl.when(pl.program_id(0) % tiles_per_seq == 0)
    def _():
        state[...] = jnp.zeros_like(state)

    units = [(ck, h) for ck in range(OUT_TILE // RET_CHUNK) for h in range(RET_HEADS)]

    def retention(selected):
        for ck, h in selected:
            _retention_unit(ck, h, q_ref, k_ref, kz_ref, v_ref, bz_ref, gain_ref, decay_ref,
                            xi_ref, state, ybuf, chunk_decay)

    ys = jnp.concatenate([ys_ref[s] for s in range(S5_SLABS)], axis=1)
    u = jnp.concatenate([u_ref[s] for s in range(S5_SLABS)], axis=1)
    y = ys.astype(F32) + d_ref[...] * u.astype(F32)
    y = jax.nn.gelu(y)
    gl = jnp.dot(y.astype(BF16), wglu_ref[...], preferred_element_type=F32) + bglu_ref[...]
    n_chunks = OUT_TILE // RET_CHUNK
    retention(units[:RET_HEADS])
    ya = (y * jax.nn.sigmoid(gl) * az_ref[...].astype(F32)).astype(BF16)
    acc = jnp.dot(ya, w_ref[0:S5_WIDTH, :], preferred_element_type=F32)
    for ck in range(n_chunks):
        rows = slice(ck * RET_CHUNK, (ck + 1) * RET_CHUNK)
        part = jnp.dot(ybuf[rows, :], w_ref[S5_WIDTH:, :], preferred_element_type=F32)
        if ck + 1 < n_chunks:
            retention(units[(ck + 1) * RET_HEADS:(ck + 2) * RET_HEADS])
        o_ref[rows, :] = x_ref[rows, :] + (acc[rows, :] + part)


def _even_out_proj(x2d, ys, u, az, q, k, kz, v, bz, d_skip, w_glu, b_glu, w_out, gain,
                   decay, xi, chunk_decay, seq):
    n = x2d.shape[0]
    tm = OUT_TILE
    assert tm % RET_CHUNK == 0 and seq % tm == 0
    tok = pl.BlockSpec((tm, D_MODEL), lambda i: (i, 0))
    slabs = pl.BlockSpec((S5_SLABS, tm, LANES), lambda i: (0, i, 0))
    return pl.pallas_call(
        functools.partial(_even_out_kernel, chunk_decay=chunk_decay, tiles_per_seq=seq // tm),
        grid=(n // tm,),
        in_specs=[tok, slabs, slabs, tok, tok, tok, tok, tok, tok,
                  _const_spec((1, S5_WIDTH)), _const_spec(w_glu.shape), _const_spec((1, S5_WIDTH)),
                  _const_spec(w_out.shape), _const_spec((1, D_MODEL)), _const_spec(decay.shape),
                  _const_spec(xi.shape)],
        out_specs=tok,
        out_shape=jax.ShapeDtypeStruct(x2d.shape, F32),
        scratch_shapes=[pltpu.VMEM((RET_HEADS, RET_DK, RET_DV), F32),
                        pltpu.VMEM((tm, D_MODEL), BF16)],
        compiler_params=pltpu.CompilerParams(
            dimension_semantics=("arbitrary",), vmem_limit_bytes=VMEM_LIMIT_BYTES),
        name="even_out_proj",
    )(x2d, ys, u, az, q, k, kz, v, bz, d_skip, w_glu, b_glu, w_out, gain, decay, xi)


def _odd_kernel(x_ref, g_ref, w_ref, vgain_ref, wsp_ref, bsp_ref, wout_ref, fg_ref, o_ref,
                vbuf, ubuf, gbuf, vnbuf, ybuf):
    x = x_ref[...]
    hn = _rms_norm(x, g_ref[...]).astype(BF16)
    gd = SGU_GROUP_DIM

    def proj(col):
        return jnp.dot(hn, w_ref[:, col:col + gd], preferred_element_type=F32)

    for g in range(SGU_GROUPS):
        vbuf[:, g * gd:(g + 1) * gd] = jax.nn.gelu(proj(SGU_WIDTH + g * gd))
    for g in range(SGU_GROUPS):
        ubuf[:, g * gd:(g + 1) * gd] = jax.nn.gelu(proj(g * gd))
        gbuf[:, g * gd:(g + 1) * gd] = _silu(proj(2 * SGU_WIDTH + g * gd))
    v = vbuf[...]
    mu = jnp.mean(v, axis=-1, keepdims=True)
    vc = v - mu
    var = jnp.mean(vc * vc, axis=-1, keepdims=True)
    vnbuf[...] = (vc * lax.rsqrt(var + NORM_EPS) * vgain_ref[...]).astype(BF16)

    row = lax.broadcasted_iota(jnp.int32, (SGU_CHUNK, SGU_CHUNK), 0)
    col = lax.broadcasted_iota(jnp.int32, (SGU_CHUNK, SGU_CHUNK), 1)
    for g in range(SGU_GROUPS):
        cols = slice(g * gd, (g + 1) * gd)
        wm = jnp.where(row >= col, wsp_ref[g], 0.0).astype(BF16)
        bias = bsp_ref[g]
        for c in range(ODD_TILE // SGU_CHUNK):
            rows = slice(c * SGU_CHUNK, (c + 1) * SGU_CHUNK)
            s = jnp.dot(wm, vnbuf[rows, cols], preferred_element_type=F32) + bias
            ybuf[rows, cols] = (ubuf[rows, cols] * s * gbuf[rows, cols]).astype(BF16)

    xn = x + jnp.dot(ybuf[...], wout_ref[...], preferred_element_type=F32)
    o_ref[...] = _rms_norm(xn, fg_ref[...])


def _odd_layer(x2d, gain, w_in, vgain, wsp, bsp, w_out, final_gain):
    n = x2d.shape[0]
    tm = ODD_TILE
    tok = pl.BlockSpec((tm, D_MODEL), lambda i: (i, 0))
    return pl.pallas_call(
        _odd_kernel,
        grid=(n // tm,),
        in_specs=[tok, _const_spec((1, D_MODEL)), _const_spec(w_in.shape),
                  _const_spec((1, SGU_WIDTH)), _const_spec(wsp.shape), _const_spec(bsp.shape),
                  _const_spec(w_out.shape), _const_spec((1, D_MODEL))],
        out_specs=tok,
        out_shape=jax.ShapeDtypeStruct(x2d.shape, F32),
        scratch_shapes=[pltpu.VMEM((tm, SGU_WIDTH), F32),
                        pltpu.VMEM((tm, SGU_WIDTH), F32),
                        pltpu.VMEM((tm, SGU_WIDTH), F32),
                        pltpu.VMEM((tm, SGU_WIDTH), BF16),
                        pltpu.VMEM((tm, SGU_WIDTH), BF16)],
        compiler_params=pltpu.CompilerParams(
            dimension_semantics=("arbitrary",), vmem_limit_bytes=VMEM_LIMIT_BYTES),
        name="odd_layer",
    )(x2d, gain, w_in, vgain, wsp, bsp, w_out, final_gain)


def _rotary_tables(seq):
    half = RET_DK // 2
    pos = np.arange(seq, dtype=np.float64)
    inv = ROPE_BASE ** (-np.arange(half, dtype=np.float64) / half)
    ang = pos[:, None] * inv[None, :]
    return jnp.asarray(np.cos(ang), F32), jnp.asarray(np.sin(ang), F32)


def kernel(x, norm_even, w_in_even, s5_lam_re, s5_lam_im, s5_log_dt, s5_b_re, s5_b_im, s5_c_re, s5_c_im, s5_d, s5_w_glu, s5_b_glu, ret_gn_gain, w_out_even, norm_odd, w_in_odd, sgu_norm_gain, sgu_w_spatial, sgu_b_spatial, w_out_odd, final_norm):
    bsz, seq, d = x.shape
    x2d = x.reshape(bsz * seq, d)
    cos, sin = _rotary_tables(seq)

    decay, xi, zeta, chunk_decay = _retention_tables()
    (u, az, q, k, kz, v, bz), (w_glu, w_out_e, w_in_o, w_out_o) = _even_in_proj(
        x2d, norm_even[0].reshape(1, d), w_in_even[0].astype(BF16), cos, sin, zeta, seq,
        [s5_w_glu[0], w_out_even[0], w_in_odd[0], w_out_odd[0]])
    seg_len = seq // S5_BLOCK // S5_SEGS
    kw, cp, bpg, a_tab = _s5_params(s5_lam_re[0], s5_lam_im[0], s5_log_dt[0], s5_b_re[0],
                                    s5_b_im[0], s5_c_re[0], s5_c_im[0], seg_len)
    ys = _s5_branch(u, kw, cp, bpg, a_tab, bsz, seq)
    x2d = _even_out_proj(x2d, ys, u, az, q, k, kz, v, bz, s5_d[0].reshape(1, -1).astype(F32),
                         w_glu, s5_b_glu[0].reshape(1, -1).astype(F32), w_out_e,
                         ret_gn_gain[0].reshape(1, -1).astype(F32), decay, xi, chunk_decay, seq)

    out = _odd_layer(x2d, norm_odd[0].reshape(1, d), w_in_o,
                     sgu_norm_gain[0].reshape(1, -1).astype(F32), sgu_w_spatial[0].astype(F32),
                     sgu_b_spatial[0].astype(F32)[:, :, None], w_out_o,
                     final_norm.reshape(1, d))
    return out.reshape(bsz, seq, d)
```

```python
import functools

import jax
import jax.numpy as jnp
import numpy as np
from jax import lax
from jax.experimental import pallas as pl
from jax.experimental.pallas import tpu as pltpu

F32 = jnp.float32
BF16 = jnp.bfloat16

D_MODEL = 1024
S5_WIDTH = 1024
S5_GROUP = 16
S5_GROUPS = 64
S5_STATE = 64
RET_HEADS = 4
RET_DK = 256
RET_DV = 256
ROPE_BASE = 10000.0
SGU_WIDTH = 2048
SGU_GROUPS = 4
SGU_GROUP_DIM = 512
SGU_CHUNK = 128
NORM_EPS = 1e-6

V7X_VMEM_BYTES = 64 * 1024 * 1024
VMEM_LIMIT_BYTES = V7X_VMEM_BYTES - 12 * 1024 * 1024
SUBLANES = 8
LANES = 128

PROJ_TILE = 512
OUT_TILE = 512
S5_BLOCK = 16
S5_SEGS = SUBLANES
S5_CHUNK_GROUPS = 8
S5_CHUNK_PAIRS = S5_CHUNK_GROUPS // 2
S5_CHUNK_LANES = S5_CHUNK_GROUPS * S5_STATE
S5_SLABS = S5_WIDTH // LANES
S5_TOK_PITCH = 24
RET_CHUNK = 256
ODD_TILE = 512


def _const_spec(shape, single_buffer=True):
    zeros = (0,) * len(shape)
    if single_buffer:
        return pl.BlockSpec(shape, lambda *_: zeros, pipeline_mode=pl.Buffered(1))
    return pl.BlockSpec(shape, lambda *_: zeros)


def _rms_norm(x, gain):
    ms = jnp.mean(x * x, axis=-1, keepdims=True)
    return x * lax.rsqrt(ms + NORM_EPS) * gain


def _silu(x):
    return x * jax.nn.sigmoid(x)


def _even_in_kernel(x_ref, g_ref, w_ref, cos_ref, sin_ref, zeta_ref, *rest):
    n_cast = (len(rest) - 7) // 2
    cast_in = rest[:n_cast]
    u_ref, az_ref, q_ref, k_ref, kz_ref, v_ref, bz_ref = rest[n_cast:n_cast + 7]
    cast_out = rest[n_cast + 7:]
    for src, dst in zip(cast_in, cast_out):
        dst[...] = src[...].astype(BF16)

    hn = _rms_norm(x_ref[...], g_ref[...]).astype(BF16)

    def proj(j):
        return jnp.dot(hn, w_ref[:, j * D_MODEL:(j + 1) * D_MODEL], preferred_element_type=F32)

    def rotary_store(p, out_ref, scale, decayed_ref=None):
        cos = cos_ref[...]
        sin = sin_ref[...]
        half = RET_DK // 2
        for h in range(RET_HEADS):
            lo = slice(h * RET_DK, h * RET_DK + half)
            hi = slice(h * RET_DK + half, (h + 1) * RET_DK)
            x1 = p[:, lo]
            x2 = p[:, hi]
            o1 = x1 * cos - x2 * sin
            o2 = x1 * sin + x2 * cos
            if scale != 1.0:
                o1 = o1 * scale
                o2 = o2 * scale
            out_ref[:, lo] = o1.astype(BF16)
            out_ref[:, hi] = o2.astype(BF16)
            if decayed_ref is not None:
                decayed_ref[:, lo] = (o1 * zeta_ref[:, lo]).astype(BF16)
                decayed_ref[:, hi] = (o2 * zeta_ref[:, hi]).astype(BF16)

    u = proj(0).astype(BF16)
    for s in range(S5_SLABS):
        u_ref[s] = u[:, s * LANES:(s + 1) * LANES]
    az_ref[...] = _silu(proj(1)).astype(BF16)
    rotary_store(proj(2), q_ref, 1.0)
    rotary_store(proj(3), k_ref, RET_DK ** -0.5, kz_ref)
    v_ref[...] = proj(4).astype(BF16)
    bz_ref[...] = _silu(proj(5)).astype(BF16)


def _even_in_proj(x2d, gain, w_in, cos, sin, zeta, seq, later_weights):
    n = x2d.shape[0]
    tm = PROJ_TILE
    steps = n // tm
    tiles_per_seq = seq // tm
    tok = pl.BlockSpec((tm, D_MODEL), lambda i: (i, 0))
    rot = pl.BlockSpec((tm, RET_DK // 2), lambda i: (i % tiles_per_seq, 0))
    out = jax.ShapeDtypeStruct((n, D_MODEL), BF16)
    cast_specs = [pl.BlockSpec((w.shape[0] // steps, w.shape[1]), lambda i: (i, 0))
                  for w in later_weights]
    res = pl.pallas_call(
        _even_in_kernel,
        grid=(steps,),
        in_specs=[tok, _const_spec((1, D_MODEL)), _const_spec(w_in.shape), rot, rot,
                  _const_spec(zeta.shape)] + cast_specs,
        out_specs=([pl.BlockSpec((S5_SLABS, tm, LANES), lambda i: (0, i, 0))] + [tok] * 6
                   + cast_specs),
        out_shape=([jax.ShapeDtypeStruct((S5_SLABS, n, LANES), BF16)] + [out] * 6
                   + [jax.ShapeDtypeStruct(w.shape, BF16) for w in later_weights]),
        compiler_params=pltpu.CompilerParams(
            dimension_semantics=("arbitrary",), vmem_limit_bytes=VMEM_LIMIT_BYTES),
        name="even_in_proj",
    )(x2d, gain, w_in, cos, sin, zeta, *later_weights)
    return res[:7], res[7:]


def _s5_kernel(u_ref, kw_ref, cp_ref, bpg_ref, pcon_ref, econ_ref, a_ref, y_ref,
               tokbuf, zt, ytbuf, vbuf, spbuf, ebuf, sinbuf, wyt, bp, *, n_blk, seg_len):
    lanes = S5_CHUNK_LANES
    re = slice(0, lanes)
    im = slice(lanes, 2 * lanes)
    pairs = S5_CHUNK_PAIRS
    blk_w = S5_BLOCK * S5_GROUP

    @pl.when(pl.program_id(1) == 0)
    def _():
        kw = kw_ref[...].reshape(S5_CHUNK_GROUPS * S5_GROUP, blk_w)
        for jo in range(S5_BLOCK):
            rows = jnp.dot(kw, pcon_ref[jo], preferred_element_type=F32).astype(BF16)
            for gl in range(S5_CHUNK_GROUPS):
                wyt[gl, jo * S5_GROUP:(jo + 1) * S5_GROUP, 0:blk_w] = (
                    rows[gl * S5_GROUP:(gl + 1) * S5_GROUP, :])
        for gl in range(S5_CHUNK_GROUPS):
            wyt[gl, :, blk_w:] = cp_ref[gl]
        for q in range(pairs):
            for par in range(2):
                bp[q, par * blk_w:(par + 1) * blk_w, :] = jnp.dot(
                    bpg_ref[2 * q + par], econ_ref[par], preferred_element_type=F32).astype(BF16)

    def pos(b):
        return (b % seg_len) * S5_SEGS + b // seg_len

    for b in range(n_blk):
        tokbuf[pos(b) * S5_TOK_PITCH:pos(b) * S5_TOK_PITCH + S5_BLOCK, :] = (
            u_ref[b * S5_BLOCK:(b + 1) * S5_BLOCK, :].astype(F32))
    for j in range(S5_BLOCK):
        ujt = tokbuf[pl.ds(j, n_blk, stride=S5_TOK_PITCH), :].astype(BF16).T
        for gl in range(S5_CHUNK_GROUPS):
            zt[gl, j * S5_GROUP:(j + 1) * S5_GROUP, :] = ujt[gl * S5_GROUP:(gl + 1) * S5_GROUP, :]

    for q in range(pairs):
        zc = jnp.concatenate([zt[2 * q], zt[2 * q + 1]], axis=0)
        v = lax.dot_general(zc, bp[q], (((0,), (0,)), ((), ())),
                            preferred_element_type=F32)
        vbuf[q] = v[:, 0:LANES]
        vbuf[pairs + q] = v[:, LANES:]

    ar = jnp.broadcast_to(a_ref[0:1, :], (SUBLANES, lanes))
    ai = jnp.broadcast_to(a_ref[1:2, :], (SUBLANES, lanes))

    def advance(sr, si, i):
        rows = pl.ds(pl.multiple_of(i * S5_SEGS, S5_SEGS), S5_SEGS)
        xr = jnp.concatenate([vbuf[s, rows, :] for s in range(pairs)], axis=1)
        xi = jnp.concatenate([vbuf[pairs + s, rows, :] for s in range(pairs)], axis=1)
        return ar * sr - ai * si + xr, ar * si + ai * sr + xi

    def end_state(i, state):
        return advance(*state, i)

    zero = jnp.zeros((SUBLANES, lanes), F32)
    er, ei = lax.fori_loop(0, seg_len, end_state, (zero, zero), unroll=8)
    ebuf[:, re] = er
    ebuf[:, im] = ei

    atr = a_ref[2:3, :]
    ati = a_ref[3:4, :]
    cr = jnp.zeros((1, lanes), F32)
    ci = jnp.zeros((1, lanes), F32)
    for seg in range(S5_SEGS):
        sinbuf[seg:seg + 1, re] = cr
        sinbuf[seg:seg + 1, im] = ci
        cr, ci = (atr * cr - ati * ci + ebuf[seg:seg + 1, re],
                  atr * ci + ati * cr + ebuf[seg:seg + 1, im])

    def record(i, state):
        sr, si = state
        rows = pl.ds(pl.multiple_of(i * S5_SEGS, S5_SEGS), S5_SEGS)
        for s in range(pairs):
            spbuf[s, rows, :] = sr[:, s * LANES:(s + 1) * LANES]
            spbuf[pairs + s, rows, :] = si[:, s * LANES:(s + 1) * LANES]
        return advance(sr, si, i)

    lax.fori_loop(0, seg_len, record, (sinbuf[:, re], sinbuf[:, im]), unroll=8)

    for q in range(pairs):
        sp = jnp.concatenate([spbuf[q], spbuf[pairs + q]], axis=1)
        spt = sp.astype(BF16).T
        for par in range(2):
            gl = 2 * q + par
            rhs = jnp.concatenate(
                [zt[gl], spt[par * S5_STATE:(par + 1) * S5_STATE],
                 spt[LANES + par * S5_STATE:LANES + (par + 1) * S5_STATE]], axis=0)
            ytbuf[gl] = jnp.dot(wyt[gl], rhs, preferred_element_type=F32)

    for j in range(S5_BLOCK):
        yt = jnp.concatenate([ytbuf[gl, j * S5_GROUP:(j + 1) * S5_GROUP, :]
                              for gl in range(S5_CHUNK_GROUPS)], axis=0)
        tokbuf[pl.ds(j, n_blk, stride=S5_TOK_PITCH), :] = yt.astype(BF16).T.astype(F32)
    for b in range(n_blk):
        y_ref[b * S5_BLOCK:(b + 1) * S5_BLOCK, :] = (
            tokbuf[pos(b) * S5_TOK_PITCH:pos(b) * S5_TOK_PITCH + S5_BLOCK, :].astype(BF16))


def _s5_params(lam_re, lam_im, log_dt, b_re, b_im, c_re, c_im, seg_len):
    hp = lax.Precision.HIGHEST
    r = S5_BLOCK
    g_n, p_n, h_n = S5_GROUPS, S5_STATE, S5_GROUP
    lr = jnp.minimum(lam_re.astype(F32), -1e-4)
    li = lam_im.astype(F32)
    dt = jnp.exp(log_dt.astype(F32))[:, None]
    mag = jnp.exp(lr * dt)
    ab_re = mag * jnp.cos(li * dt)
    ab_im = mag * jnp.sin(li * dt)
    den = lr * lr + li * li
    n_re = ab_re - 1.0
    n_im = ab_im
    z_re = (n_re * lr + n_im * li) / den
    z_im = (n_im * lr - n_re * li) / den
    br = b_re.astype(F32)
    bi = b_im.astype(F32)
    bb_re = z_re[..., None] * br - z_im[..., None] * bi
    bb_im = z_re[..., None] * bi + z_im[..., None] * br

    ldt = (lr * dt)[:, None, :]
    wdt = (li * dt)[:, None, :]

    def a_pow(k):
        pm = jnp.exp(ldt * k)
        return pm * jnp.cos(wdt * k), pm * jnp.sin(wdt * k)

    pr, pi = a_pow(jnp.arange(r + 1, dtype=F32)[None, :, None])
    cr = c_re.astype(F32)
    ci = c_im.astype(F32)

    car_re = cr[:, :, None, :] * pr[:, None, :r] - ci[:, :, None, :] * pi[:, None, :r]
    car_im = cr[:, :, None, :] * pi[:, None, :r] + ci[:, :, None, :] * pr[:, None, :r]
    kt = (jnp.einsum('galp,gph->galh', car_re, bb_re, precision=hp)
          - jnp.einsum('galp,gph->galh', car_im, bb_im, precision=hp))
    kw = kt.reshape(g_n, h_n, r * h_n).astype(BF16)

    cre = (cr[:, None] * pr[:, 1:, None, :] - ci[:, None] * pi[:, 1:, None, :]).reshape(
        g_n, r * h_n, p_n)
    cim = -(cr[:, None] * pi[:, 1:, None, :] + ci[:, None] * pr[:, 1:, None, :]).reshape(
        g_n, r * h_n, p_n)
    cp = jnp.concatenate([cre, cim], axis=2).astype(BF16)

    bbr = jnp.transpose(bb_re, (0, 2, 1))[:, None]
    bbi = jnp.transpose(bb_im, (0, 2, 1))[:, None]
    prr = pr[:, r - 1::-1][:, :, None, :]
    pir = pi[:, r - 1::-1][:, :, None, :]
    bpg = jnp.concatenate([(prr * bbr - pir * bbi).reshape(g_n, r * h_n, p_n),
                           (prr * bbi + pir * bbr).reshape(g_n, r * h_n, p_n)],
                          axis=2).astype(BF16)

    at_re, at_im = a_pow(float(r * seg_len))
    n_chunks = g_n // S5_CHUNK_GROUPS
    a_tab = jnp.stack([x.reshape(n_chunks, S5_CHUNK_LANES)
                       for x in (pr[:, r], pi[:, r], at_re, at_im)], axis=1)
    return kw, cp, bpg, a_tab


def _s5_placement_tables():
    r, h_n, p_n = S5_BLOCK, S5_GROUP, S5_STATE
    pcon = np.zeros((r, r, h_n, r, h_n), np.float32)
    for jo in range(r):
        for ji in range(jo + 1):
            pcon[jo, jo - ji, np.arange(h_n), ji, np.arange(h_n)] = 1.0
    econ = np.zeros((2, 2 * p_n, 4 * p_n), np.float32)
    for par in range(2):
        econ[par, np.arange(p_n), par * p_n + np.arange(p_n)] = 1.0
        econ[par, p_n + np.arange(p_n), (2 + par) * p_n + np.arange(p_n)] = 1.0
    return (jnp.asarray(pcon.reshape(r, r * h_n, r * h_n), BF16), jnp.asarray(econ, BF16))


def _s5_branch(u, kw, cp, bpg, a_tab, bsz, seq):
    n_blk = seq // S5_BLOCK
    seg_len = n_blk // S5_SEGS
    width = S5_BLOCK * S5_GROUP
    slab = S5_CHUNK_GROUPS * S5_GROUP
    assert slab == LANES and S5_GROUPS // S5_CHUNK_GROUPS == S5_SLABS
    pcon, econ = _s5_placement_tables()
    tok = pl.BlockSpec((None, seq, slab), lambda c, b: (c, b, 0))

    def per_chunk(shape):
        return pl.BlockSpec((S5_CHUNK_GROUPS,) + shape, lambda c, b: (c, 0, 0))

    return pl.pallas_call(
        functools.partial(_s5_kernel, n_blk=n_blk, seg_len=seg_len),
        grid=(S5_GROUPS // S5_CHUNK_GROUPS, bsz),
        in_specs=[tok, per_chunk((S5_GROUP, width)), per_chunk((width, 2 * S5_STATE)),
                  per_chunk((width, 2 * S5_STATE)), _const_spec(pcon.shape), _const_spec(econ.shape),
                  pl.BlockSpec((None, 4, S5_CHUNK_LANES), lambda c, b: (c, 0, 0))],
        out_specs=tok,
        out_shape=jax.ShapeDtypeStruct(u.shape, BF16),
        scratch_shapes=[
            pltpu.VMEM((n_blk * S5_TOK_PITCH, slab), F32),
            pltpu.VMEM((S5_CHUNK_GROUPS, width, n_blk), BF16),
            pltpu.VMEM((S5_CHUNK_GROUPS, width, n_blk), F32),
            pltpu.VMEM((2 * S5_CHUNK_PAIRS, n_blk, LANES), F32),
            pltpu.VMEM((2 * S5_CHUNK_PAIRS, n_blk, LANES), F32),
            pltpu.VMEM((S5_SEGS, 2 * S5_CHUNK_LANES), F32),
            pltpu.VMEM((S5_SEGS, 2 * S5_CHUNK_LANES), F32),
            pltpu.VMEM((S5_CHUNK_GROUPS, width, width + 2 * S5_STATE), BF16),
            pltpu.VMEM((S5_CHUNK_PAIRS, 2 * width, 4 * S5_STATE), BF16),
        ],
        compiler_params=pltpu.CompilerParams(
            dimension_semantics=("arbitrary", "arbitrary"), vmem_limit_bytes=VMEM_LIMIT_BYTES),
        name="s5_branch",
    )(u, kw, cp, bpg, pcon, econ, a_tab)


def _retention_unit(ck, h, q_ref, k_ref, kz_ref, v_ref, bz_ref, gain_ref, decay_ref, xi_ref,
                    state, yb_ref, chunk_decay):
    rows = slice(ck * RET_CHUNK, (ck + 1) * RET_CHUNK)
    cols = slice(h * RET_DK, (h + 1) * RET_DK)
    qh = q_ref[rows, cols]
    vh = v_ref[rows, cols]
    scores = lax.dot_general(qh, k_ref[rows, cols], (((1,), (1,)), ((), ())),
                             preferred_element_type=F32)
    inner = jnp.dot((scores * decay_ref[h]).astype(BF16), vh, preferred_element_type=F32)
    prev = state[h]
    cross = jnp.dot(qh, prev.astype(BF16), preferred_element_type=F32) * xi_ref[:, cols]
    local = lax.dot_general(kz_ref[rows, cols], vh, (((0,), (0,)), ((), ())),
                            preferred_element_type=F32)
    state[h] = prev * chunk_decay[h] + local
    o = inner + cross
    mu = jnp.mean(o, axis=-1, keepdims=True)
    oc = o - mu
    var = jnp.mean(oc * oc, axis=-1, keepdims=True)
    o = oc * lax.rsqrt(var + NORM_EPS) * gain_ref[:, cols]
    yb_ref[rows, cols] = (o * bz_ref[rows, cols].astype(F32)).astype(BF16)


def _retention_tables():
    log_g = np.log1p(-np.exp2(-5.0 - np.arange(RET_HEADS, dtype=np.float64)))
    idx = np.arange(RET_CHUNK, dtype=np.float64)
    diff = idx[:, None] - idx[None, :]
    decay = np.where(diff >= 0, np.exp(log_g[:, None, None] * np.maximum(diff, 0.0)), 0.0)
    xi = np.repeat(np.exp(log_g[None, :] * (idx[:, None] + 1.0)), RET_DK, axis=1)
    zeta = np.repeat(np.exp(log_g[None, :] * (RET_CHUNK - 1.0 - idx[:, None])), RET_DK, axis=1)
    zeta = np.tile(zeta, (PROJ_TILE // RET_CHUNK, 1))
    chunk_decay = tuple(float(x) for x in np.exp(log_g * RET_CHUNK))
    return (jnp.asarray(decay, F32), jnp.asarray(xi, F32), jnp.asarray(zeta, F32), chunk_decay)


def _even_out_kernel(x_ref, ys_ref, u_ref, az_ref, q_ref, k_ref, kz_ref, v_ref, bz_ref,
                     d_ref, wglu_ref, bglu_ref, w_ref, gain_ref, decay_ref, xi_ref,
                     o_ref, state, ybuf, *, chunk_decay, tiles_per_seq):
    @pl.when(pl.program_id(0) % tiles_per_seq == 0)
    def _():
        state[...] = jnp.zeros_like(state)

    units = [(ck, h) for ck in range(OUT_TILE // RET_CHUNK) for h in range(RET_HEADS)]

    def retention(selected):
        for ck, h in selected:
            _retention_unit(ck, h, q_ref, k_ref, kz_ref, v_ref, bz_ref, gain_ref, decay_ref,
                            xi_ref, state, ybuf, chunk_decay)

    ys = jnp.concatenate([ys_ref[s] for s in range(S5_SLABS)], axis=1)
    u = jnp.concatenate([u_ref[s] for s in range(S5_SLABS)], axis=1)
    y = ys.astype(F32) + d_ref[...] * u.astype(F32)
    y = jax.nn.gelu(y)
    gl = jnp.dot(y.astype(BF16), wglu_ref[...], preferred_element_type=F32) + bglu_ref[...]
    n_chunks = OUT_TILE // RET_CHUNK
    retention(units[:RET_HEADS])
    ya = (y * jax.nn.sigmoid(gl) * az_ref[...].astype(F32)).astype(BF16)
    acc = jnp.dot(ya, w_ref[0:S5_WIDTH, :], preferred_element_type=F32)
    for ck in range(n_chunks):
        rows = slice(ck * RET_CHUNK, (ck + 1) * RET_CHUNK)
        part = jnp.dot(ybuf[rows, :], w_ref[S5_WIDTH:, :], preferred_element_type=F32)
        if ck + 1 < n_chunks:
            retention(units[(ck + 1) * RET_HEADS:(ck + 2) * RET_HEADS])
        o_ref[rows, :] = x_ref[rows, :] + (acc[rows, :] + part)


def _even_out_proj(x2d, ys, u, az, q, k, kz, v, bz, d_skip, w_glu, b_glu, w_out, gain,
                   decay, xi, chunk_decay, seq):
    n = x2d.shape[0]
    tm = OUT_TILE
    assert tm % RET_CHUNK == 0 and seq % tm == 0
    tok = pl.BlockSpec((tm, D_MODEL), lambda i: (i, 0))
    slabs = pl.BlockSpec((S5_SLABS, tm, LANES), lambda i: (0, i, 0))
    return pl.pallas_call(
        functools.partial(_even_out_kernel, chunk_decay=chunk_decay, tiles_per_seq=seq // tm),
        grid=(n // tm,),
        in_specs=[tok, slabs, slabs, tok, tok, tok, tok, tok, tok,
                  _const_spec((1, S5_WIDTH)), _const_spec(w_glu.shape), _const_spec((1, S5_WIDTH)),
                  _const_spec(w_out.shape), _const_spec((1, D_MODEL)), _const_spec(decay.shape),
                  _const_spec(xi.shape)],
        out_specs=tok,
        out_shape=jax.ShapeDtypeStruct(x2d.shape, F32),
        scratch_shapes=[pltpu.VMEM((RET_HEADS, RET_DK, RET_DV), F32),
                        pltpu.VMEM((tm, D_MODEL), BF16)],
        compiler_params=pltpu.CompilerParams(
            dimension_semantics=("arbitrary",), vmem_limit_bytes=VMEM_LIMIT_BYTES),
        name="even_out_proj",
    )(x2d, ys, u, az, q, k, kz, v, bz, d_skip, w_glu, b_glu, w_out, gain, decay, xi)


def _odd_kernel(x_ref, g_ref, w_ref, vgain_ref, wsp_ref, bsp_ref, wout_ref, fg_ref, o_ref,
                vbuf, ubuf, gbuf, vnbuf, ybuf):
    x = x_ref[...]
    hn = _rms_norm(x, g_ref[...]).astype(BF16)
    gd = SGU_GROUP_DIM

    def proj(col):
        return jnp.dot(hn, w_ref[:, col:col + gd], preferred_element_type=F32)

    for g in range(SGU_GROUPS):
        vbuf[:, g * gd:(g + 1) * gd] = jax.nn.gelu(proj(SGU_WIDTH + g * gd))
    for g in range(SGU_GROUPS):
        ubuf[:, g * gd:(g + 1) * gd] = jax.nn.gelu(proj(g * gd))
        gbuf[:, g * gd:(g + 1) * gd] = _silu(proj(2 * SGU_WIDTH + g * gd))
    v = vbuf[...]
    mu = jnp.mean(v, axis=-1, keepdims=True)
    vc = v - mu
    var = jnp.mean(vc * vc, axis=-1, keepdims=True)
    vnbuf[...] = (vc * lax.rsqrt(var + NORM_EPS) * vgain_ref[...]).astype(BF16)

    row = lax.broadcasted_iota(jnp.int32, (SGU_CHUNK, SGU_CHUNK), 0)
    col = lax.broadcasted_iota(jnp.int32, (SGU_CHUNK, SGU_CHUNK), 1)
    for g in range(SGU_GROUPS):
        cols = slice(g * gd, (g + 1) * gd)
        wm = jnp.where(row >= col, wsp_ref[g], 0.0).astype(BF16)
        bias = bsp_ref[g]
        for c in range(ODD_TILE // SGU_CHUNK):
            rows = slice(c * SGU_CHUNK, (c + 1) * SGU_CHUNK)
            s = jnp.dot(wm, vnbuf[rows, cols], preferred_element_type=F32) + bias
            ybuf[rows, cols] = (ubuf[rows, cols] * s * gbuf[rows, cols]).astype(BF16)

    xn = x + jnp.dot(ybuf[...], wout_ref[...], preferred_element_type=F32)
    o_ref[...] = _rms_norm(xn, fg_ref[...])


def _odd_layer(x2d, gain, w_in, vgain, wsp, bsp, w_out, final_gain):
    n = x2d.shape[0]
    tm = ODD_TILE
    tok = pl.BlockSpec((tm, D_MODEL), lambda i: (i, 0))
    return pl.pallas_call(
        _odd_kernel,
        grid=(n // tm,),
        in_specs=[tok, _const_spec((1, D_MODEL)), _const_spec(w_in.shape),
                  _const_spec((1, SGU_WIDTH)), _const_spec(wsp.shape), _const_spec(bsp.shape),
                  _const_spec(w_out.shape), _const_spec((1, D_MODEL))],
        out_specs=tok,
        out_shape=jax.ShapeDtypeStruct(x2d.shape, F32),
        scratch_shapes=[pltpu.VMEM((tm, SGU_WIDTH), F32),
                        pltpu.VMEM((tm, SGU_WIDTH), F32),
                        pltpu.VMEM((tm, SGU_WIDTH), F32),
                        pltpu.VMEM((tm, SGU_WIDTH), BF16),
                        pltpu.VMEM((tm, SGU_WIDTH), BF16)],
        compiler_params=pltpu.CompilerParams(
            dimension_semantics=("arbitrary",), vmem_limit_bytes=VMEM_LIMIT_BYTES),
        name="odd_layer",
    )(x2d, gain, w_in, vgain, wsp, bsp, w_out, final_gain)


def _rotary_tables(seq):
    half = RET_DK // 2
    pos = np.arange(seq, dtype=np.float64)
    inv = ROPE_BASE ** (-np.arange(half, dtype=np.float64) / half)
    ang = pos[:, None] * inv[None, :]
    return jnp.asarray(np.cos(ang), F32), jnp.asarray(np.sin(ang), F32)


def kernel(x, norm_even, w_in_even, s5_lam_re, s5_lam_im, s5_log_dt, s5_b_re, s5_b_im, s5_c_re, s5_c_im, s5_d, s5_w_glu, s5_b_glu, ret_gn_gain, w_out_even, norm_odd, w_in_odd, sgu_norm_gain, sgu_w_spatial, sgu_b_spatial, w_out_odd, final_norm):
    bsz, seq, d = x.shape
    x2d = x.reshape(bsz * seq, d)
    cos, sin = _rotary_tables(seq)

    decay, xi, zeta, chunk_decay = _retention_tables()
    (u, az, q, k, kz, v, bz), (w_glu, w_out_e, w_in_o, w_out_o) = _even_in_proj(
        x2d, norm_even[0].reshape(1, d), w_in_even[0].astype(BF16), cos, sin, zeta, seq,
        [s5_w_glu[0], w_out_even[0], w_in_odd[0], w_out_odd[0]])
    seg_len = seq // S5_BLOCK // S5_SEGS
    kw, cp, bpg, a_tab = _s5_params(s5_lam_re[0], s5_lam_im[0], s5_log_dt[0], s5_b_re[0],
                                    s5_b_im[0], s5_c_re[0], s5_c_im[0], seg_len)
    ys = _s5_branch(u, kw, cp, bpg, a_tab, bsz, seq)
    x2d = _even_out_proj(x2d, ys, u, az, q, k, kz, v, bz, s5_d[0].reshape(1, -1).astype(F32),
                         w_glu, s5_b_glu[0].reshape(1, -1).astype(F32), w_out_e,
                         ret_gn_gain[0].reshape(1, -1).astype(F32), decay, xi, chunk_decay, seq)

    out = _odd_layer(x2d, norm_odd[0].reshape(1, d), w_in_o,
                     sgu_norm_gain[0].reshape(1, -1).astype(F32), sgu_w_spatial[0].astype(F32),
                     sgu_b_spatial[0].astype(F32)[:, :, None], w_out_o,
                     final_norm.reshape(1, d))
    return out.reshape(bsz, seq, d)
```

```python
import functools

import jax
import jax.numpy as jnp
import numpy as np
from jax import lax
from jax.experimental import pallas as pl
from jax.experimental.pallas import tpu as pltpu

F32 = jnp.float32
BF16 = jnp.bfloat16

D_MODEL = 1024
S5_WIDTH = 1024
S5_GROUP = 16
S5_GROUPS = 64
S5_STATE = 64
RET_HEADS = 4
RET_DK = 256
RET_DV = 256
ROPE_BASE = 10000.0
SGU_WIDTH = 2048
SGU_GROUPS = 4
SGU_GROUP_DIM = 512
SGU_CHUNK = 128
NORM_EPS = 1e-6

V7X_VMEM_BYTES = 64 * 1024 * 1024
VMEM_LIMIT_BYTES = V7X_VMEM_BYTES - 12 * 1024 * 1024
SUBLANES = 8
LANES = 128

PROJ_TILE = 512
W_STAGE_COLS = 512
OUT_TILE = 512
S5_BLOCK = 16
S5_SEGS = SUBLANES
S5_CHUNK_GROUPS = 8
S5_CHUNK_PAIRS = S5_CHUNK_GROUPS // 2
S5_CHUNK_LANES = S5_CHUNK_GROUPS * S5_STATE
S5_SLABS = S5_WIDTH // LANES
S5_TOK_PITCH = 24
RET_CHUNK = 256
ODD_TILE = 512


def _const_spec(shape, single_buffer=True):
    zeros = (0,) * len(shape)
    if single_buffer:
        return pl.BlockSpec(shape, lambda *_: zeros, pipeline_mode=pl.Buffered(1))
    return pl.BlockSpec(shape, lambda *_: zeros)


def _rms_norm(x, gain):
    ms = jnp.mean(x * x, axis=-1, keepdims=True)
    return x * lax.rsqrt(ms + NORM_EPS) * gain


def _silu(x):
    return x * jax.nn.sigmoid(x)


def _even_in_kernel(x_ref, g_ref, w_hbm, cos_ref, sin_ref, zeta_ref, *rest):
    n_cast = (len(rest) - 10) // 2
    cast_in = rest[:n_cast]
    u_ref, az_ref, q_ref, k_ref, kz_ref, v_ref, bz_ref = rest[n_cast:n_cast + 7]
    cast_out = rest[n_cast + 7:2 * n_cast + 7]
    w_ref, stage, sem = rest[2 * n_cast + 7:]

    @pl.when(pl.program_id(0) == 0)
    def _():
        n_blocks = w_ref.shape[1] // W_STAGE_COLS

        def block_copy(c):
            return pltpu.make_async_copy(
                w_hbm.at[:, pl.ds(c * W_STAGE_COLS, W_STAGE_COLS)], stage.at[c % 2], sem.at[c % 2])

        block_copy(0).start()
        for c in range(n_blocks):
            if c + 1 < n_blocks:
                block_copy(c + 1).start()
            block_copy(c).wait()
            w_ref[:, c * W_STAGE_COLS:(c + 1) * W_STAGE_COLS] = stage[c % 2].astype(BF16)

    for src, dst in zip(cast_in, cast_out):
        dst[...] = src[...].astype(BF16)

    hn = _rms_norm(x_ref[...], g_ref[...]).astype(BF16)

    def proj(j):
        return jnp.dot(hn, w_ref[:, j * D_MODEL:(j + 1) * D_MODEL], preferred_element_type=F32)

    def rotary_store(p, out_ref, scale, decayed_ref=None):
        cos = cos_ref[...]
        sin = sin_ref[...]
        half = RET_DK // 2
        for h in range(RET_HEADS):
            lo = slice(h * RET_DK, h * RET_DK + half)
            hi = slice(h * RET_DK + half, (h + 1) * RET_DK)
            x1 = p[:, lo]
            x2 = p[:, hi]
            o1 = x1 * cos - x2 * sin
            o2 = x1 * sin + x2 * cos
            if scale != 1.0:
                o1 = o1 * scale
                o2 = o2 * scale
            out_ref[:, lo] = o1.astype(BF16)
            out_ref[:, hi] = o2.astype(BF16)
            if decayed_ref is not None:
                decayed_ref[:, lo] = (o1 * zeta_ref[:, lo]).astype(BF16)
                decayed_ref[:, hi] = (o2 * zeta_ref[:, hi]).astype(BF16)

    u = proj(0).astype(BF16)
    for s in range(S5_SLABS):
        u_ref[s] = u[:, s * LANES:(s + 1) * LANES]
    az_ref[...] = _silu(proj(1)).astype(BF16)
    rotary_store(proj(2), q_ref, 1.0)
    rotary_store(proj(3), k_ref, RET_DK ** -0.5, kz_ref)
    v_ref[...] = proj(4).astype(BF16)
    bz_ref[...] = _silu(proj(5)).astype(BF16)


def _even_in_proj(x2d, gain, w_in, cos, sin, zeta, seq, later_weights):
    n = x2d.shape[0]
    tm = PROJ_TILE
    steps = n // tm
    tiles_per_seq = seq // tm
    tok = pl.BlockSpec((tm, D_MODEL), lambda i: (i, 0))
    rot = pl.BlockSpec((tm, RET_DK // 2), lambda i: (i % tiles_per_seq, 0))
    out = jax.ShapeDtypeStruct((n, D_MODEL), BF16)
    cast_specs = [pl.BlockSpec((w.shape[0] // steps, w.shape[1]), lambda i: (i, 0))
                  for w in later_weights]
    res = pl.pallas_call(
        _even_in_kernel,
        grid=(steps,),
        in_specs=[tok, _const_spec((1, D_MODEL)), pl.BlockSpec(memory_space=pl.ANY), rot, rot,
                  _const_spec(zeta.shape)] + cast_specs,
        out_specs=([pl.BlockSpec((S5_SLABS, tm, LANES), lambda i: (0, i, 0))] + [tok] * 6
                   + cast_specs),
        out_shape=([jax.ShapeDtypeStruct((S5_SLABS, n, LANES), BF16)] + [out] * 6
                   + [jax.ShapeDtypeStruct(w.shape, BF16) for w in later_weights]),
        scratch_shapes=[pltpu.VMEM(w_in.shape, BF16),
                        pltpu.VMEM((2, w_in.shape[0], W_STAGE_COLS), F32),
                        pltpu.SemaphoreType.DMA((2,))],
        compiler_params=pltpu.CompilerParams(
            dimension_semantics=("arbitrary",), vmem_limit_bytes=VMEM_LIMIT_BYTES),
        name="even_in_proj",
    )(x2d, gain, w_in, cos, sin, zeta, *later_weights)
    return res[:7], res[7:]


def _s5_kernel(u_ref, kw_ref, cp_ref, bpg_ref, pcon_ref, econ_ref, a_ref, y_ref,
               tokbuf, zt, ytbuf, vbuf, spbuf, ebuf, sinbuf, wyt, bp, *, n_blk, seg_len):
    lanes = S5_CHUNK_LANES
    re = slice(0, lanes)
    im = slice(lanes, 2 * lanes)
    pairs = S5_CHUNK_PAIRS
    blk_w = S5_BLOCK * S5_GROUP

    @pl.when(pl.program_id(1) == 0)
    def _():
        kw = kw_ref[...].reshape(S5_CHUNK_GROUPS * S5_GROUP, blk_w)
        for jo in range(S5_BLOCK):
            rows = jnp.dot(kw, pcon_ref[jo], preferred_element_type=F32).astype(BF16)
            for gl in range(S5_CHUNK_GROUPS):
                wyt[gl, jo * S5_GROUP:(jo + 1) * S5_GROUP, 0:blk_w] = (
                    rows[gl * S5_GROUP:(gl + 1) * S5_GROUP, :])
        for gl in range(S5_CHUNK_GROUPS):
            wyt[gl, :, blk_w:] = cp_ref[gl]
        for q in range(pairs):
            for par in range(2):
                bp[q, par * blk_w:(par + 1) * blk_w, :] = jnp.dot(
                    bpg_ref[2 * q + par], econ_ref[par], preferred_element_type=F32).astype(BF16)

    def pos(b):
        return (b % seg_len) * S5_SEGS + b // seg_len

    for b in range(n_blk):
        tokbuf[pos(b) * S5_TOK_PITCH:pos(b) * S5_TOK_PITCH + S5_BLOCK, :] = (
            u_ref[b * S5_BLOCK:(b + 1) * S5_BLOCK, :].astype(F32))
    for j in range(S5_BLOCK):
        ujt = tokbuf[pl.ds(j, n_blk, stride=S5_TOK_PITCH), :].astype(BF16).T
        for gl in range(S5_CHUNK_GROUPS):
            zt[gl, j * S5_GROUP:(j + 1) * S5_GROUP, :] = ujt[gl * S5_GROUP:(gl + 1) * S5_GROUP, :]

    for q in range(pairs):
        zc = jnp.concatenate([zt[2 * q], zt[2 * q + 1]], axis=0)
        v = lax.dot_general(zc, bp[q], (((0,), (0,)), ((), ())),
                            preferred_element_type=F32)
        vbuf[q] = v[:, 0:LANES]
        vbuf[pairs + q] = v[:, LANES:]

    ar = jnp.broadcast_to(a_ref[0:1, :], (SUBLANES, lanes))
    ai = jnp.broadcast_to(a_ref[1:2, :], (SUBLANES, lanes))

    def advance(sr, si, i):
        rows = pl.ds(pl.multiple_of(i * S5_SEGS, S5_SEGS), S5_SEGS)
        xr = jnp.concatenate([vbuf[s, rows, :] for s in range(pairs)], axis=1)
        xi = jnp.concatenate([vbuf[pairs + s, rows, :] for s in range(pairs)], axis=1)
        return ar * sr - ai * si + xr, ar * si + ai * sr + xi

    def end_state(i, state):
        return advance(*state, i)

    zero = jnp.zeros((SUBLANES, lanes), F32)
    er, ei = lax.fori_loop(0, seg_len, end_state, (zero, zero), unroll=8)
    ebuf[:, re] = er
    ebuf[:, im] = ei

    atr = a_ref[2:3, :]
    ati = a_ref[3:4, :]
    cr = jnp.zeros((1, lanes), F32)
    ci = jnp.zeros((1, lanes), F32)
    for seg in range(S5_SEGS):
        sinbuf[seg:seg + 1, re] = cr
        sinbuf[seg:seg + 1, im] = ci
        cr, ci = (atr * cr - ati * ci + ebuf[seg:seg + 1, re],
                  atr * ci + ati * cr + ebuf[seg:seg + 1, im])

    def record(i, state):
        sr, si = state
        rows = pl.ds(pl.multiple_of(i * S5_SEGS, S5_SEGS), S5_SEGS)
        for s in range(pairs):
            spbuf[s, rows, :] = sr[:, s * LANES:(s + 1) * LANES]
            spbuf[pairs + s, rows, :] = si[:, s * LANES:(s + 1) * LANES]
        return advance(sr, si, i)

    lax.fori_loop(0, seg_len, record, (sinbuf[:, re], sinbuf[:, im]), unroll=8)

    for q in range(pairs):
        sp = jnp.concatenate([spbuf[q], spbuf[pairs + q]], axis=1)
        spt = sp.astype(BF16).T
        for par in range(2):
            gl = 2 * q + par
            rhs = jnp.concatenate(
                [zt[gl], spt[par * S5_STATE:(par + 1) * S5_STATE],
                 spt[LANES + par * S5_STATE:LANES + (par + 1) * S5_STATE]], axis=0)
            ytbuf[gl] = jnp.dot(wyt[gl], rhs, preferred_element_type=F32)

    for j in range(S5_BLOCK):
        yt = jnp.concatenate([ytbuf[gl, j * S5_GROUP:(j + 1) * S5_GROUP, :]
                              for gl in range(S5_CHUNK_GROUPS)], axis=0)
        tokbuf[pl.ds(j, n_blk, stride=S5_TOK_PITCH), :] = yt.astype(BF16).T.astype(F32)
    for b in range(n_blk):
        y_ref[b * S5_BLOCK:(b + 1) * S5_BLOCK, :] = (
            tokbuf[pos(b) * S5_TOK_PITCH:pos(b) * S5_TOK_PITCH + S5_BLOCK, :].astype(BF16))


def _s5_params(lam_re, lam_im, log_dt, b_re, b_im, c_re, c_im, seg_len):
    hp = lax.Precision.HIGHEST
    r = S5_BLOCK
    g_n, p_n, h_n = S5_GROUPS, S5_STATE, S5_GROUP
    lr = jnp.minimum(lam_re.astype(F32), -1e-4)
    li = lam_im.astype(F32)
    dt = jnp.exp(log_dt.astype(F32))[:, None]
    mag = jnp.exp(lr * dt)
    ab_re = mag * jnp.cos(li * dt)
    ab_im = mag * jnp.sin(li * dt)
    den = lr * lr + li * li
    n_re = ab_re - 1.0
    n_im = ab_im
    z_re = (n_re * lr + n_im * li) / den
    z_im = (n_im * lr - n_re * li) / den
    br = b_re.astype(F32)
    bi = b_im.astype(F32)
    bb_re = z_re[..., None] * br - z_im[..., None] * bi
    bb_im = z_re[..., None] * bi + z_im[..., None] * br

    ldt = (lr * dt)[:, None, :]
    wdt = (li * dt)[:, None, :]

    def a_pow(k):
        pm = jnp.exp(ldt * k)
        return pm * jnp.cos(wdt * k), pm * jnp.sin(wdt * k)

    pr, pi = a_pow(jnp.arange(r + 1, dtype=F32)[None, :, None])
    cr = c_re.astype(F32)
    ci = c_im.astype(F32)

    car_re = cr[:, :, None, :] * pr[:, None, :r] - ci[:, :, None, :] * pi[:, None, :r]
    car_im = cr[:, :, None, :] * pi[:, None, :r] + ci[:, :, None, :] * pr[:, None, :r]
    kt = (jnp.einsum('galp,gph->galh', car_re, bb_re, precision=hp)
          - jnp.einsum('galp,gph->galh', car_im, bb_im, precision=hp))
    kw = kt.reshape(g_n, h_n, r * h_n).astype(BF16)

    cre = (cr[:, None] * pr[:, 1:, None, :] - ci[:, None] * pi[:, 1:, None, :]).reshape(
        g_n, r * h_n, p_n)
    cim = -(cr[:, None] * pi[:, 1:, None, :] + ci[:, None] * pr[:, 1:, None, :]).reshape(
        g_n, r * h_n, p_n)
    cp = jnp.concatenate([cre, cim], axis=2).astype(BF16)

    bbr = jnp.transpose(bb_re, (0, 2, 1))[:, None]
    bbi = jnp.transpose(bb_im, (0, 2, 1))[:, None]
    prr = pr[:, r - 1::-1][:, :, None, :]
    pir = pi[:, r - 1::-1][:, :, None, :]
    bpg = jnp.concatenate([(prr * bbr - pir * bbi).reshape(g_n, r * h_n, p_n),
                           (prr * bbi + pir * bbr).reshape(g_n, r * h_n, p_n)],
                          axis=2).astype(BF16)

    at_re, at_im = a_pow(float(r * seg_len))
    n_chunks = g_n // S5_CHUNK_GROUPS
    a_tab = jnp.stack([x.reshape(n_chunks, S5_CHUNK_LANES)
                       for x in (pr[:, r], pi[:, r], at_re, at_im)], axis=1)
    return kw, cp, bpg, a_tab


def _s5_placement_tables():
    r, h_n, p_n = S5_BLOCK, S5_GROUP, S5_STATE
    pcon = np.zeros((r, r, h_n, r, h_n), np.float32)
    for jo in range(r):
        for ji in range(jo + 1):
            pcon[jo, jo - ji, np.arange(h_n), ji, np.arange(h_n)] = 1.0
    econ = np.zeros((2, 2 * p_n, 4 * p_n), np.float32)
    for par in range(2):
        econ[par, np.arange(p_n), par * p_n + np.arange(p_n)] = 1.0
        econ[par, p_n + np.arange(p_n), (2 + par) * p_n + np.arange(p_n)] = 1.0
    return (jnp.asarray(pcon.reshape(r, r * h_n, r * h_n), BF16), jnp.asarray(econ, BF16))


def _s5_branch(u, kw, cp, bpg, a_tab, bsz, seq):
    n_blk = seq // S5_BLOCK
    seg_len = n_blk // S5_SEGS
    width = S5_BLOCK * S5_GROUP
    slab = S5_CHUNK_GROUPS * S5_GROUP
    assert slab == LANES and S5_GROUPS // S5_CHUNK_GROUPS == S5_SLABS
    pcon, econ = _s5_placement_tables()
    tok = pl.BlockSpec((None, seq, slab), lambda c, b: (c, b, 0))

    def per_chunk(shape):
        return pl.BlockSpec((S5_CHUNK_GROUPS,) + shape, lambda c, b: (c, 0, 0))

    return pl.pallas_call(
        functools.partial(_s5_kernel, n_blk=n_blk, seg_len=seg_len),
        grid=(S5_GROUPS // S5_CHUNK_GROUPS, bsz),
        in_specs=[tok, per_chunk((S5_GROUP, width)), per_chunk((width, 2 * S5_STATE)),
                  per_chunk((width, 2 * S5_STATE)), _const_spec(pcon.shape), _const_spec(econ.shape),
                  pl.BlockSpec((None, 4, S5_CHUNK_LANES), lambda c, b: (c, 0, 0))],
        out_specs=tok,
        out_shape=jax.ShapeDtypeStruct(u.shape, BF16),
        scratch_shapes=[
            pltpu.VMEM((n_blk * S5_TOK_PITCH, slab), F32),
            pltpu.VMEM((S5_CHUNK_GROUPS, width, n_blk), BF16),
            pltpu.VMEM((S5_CHUNK_GROUPS, width, n_blk), F32),
            pltpu.VMEM((2 * S5_CHUNK_PAIRS, n_blk, LANES), F32),
            pltpu.VMEM((2 * S5_CHUNK_PAIRS, n_blk, LANES), F32),
            pltpu.VMEM((S5_SEGS, 2 * S5_CHUNK_LANES), F32),
            pltpu.VMEM((S5_SEGS, 2 * S5_CHUNK_LANES), F32),
            pltpu.VMEM((S5_CHUNK_GROUPS, width, width + 2 * S5_STATE), BF16),
            pltpu.VMEM((S5_CHUNK_PAIRS, 2 * width, 4 * S5_STATE), BF16),
        ],
        compiler_params=pltpu.CompilerParams(
            dimension_semantics=("arbitrary", "arbitrary"), vmem_limit_bytes=VMEM_LIMIT_BYTES),
        name="s5_branch",
    )(u, kw, cp, bpg, pcon, econ, a_tab)


def _retention_unit(ck, h, q_ref, k_ref, kz_ref, v_ref, bz_ref, gain_ref, decay_ref, xi_ref,
                    state, yb_ref, chunk_decay):
    rows = slice(ck * RET_CHUNK, (ck + 1) * RET_CHUNK)
    cols = slice(h * RET_DK, (h + 1) * RET_DK)
    qh = q_ref[rows, cols]
    vh = v_ref[rows, cols]
    scores = lax.dot_general(qh, k_ref[rows, cols], (((1,), (1,)), ((), ())),
                             preferred_element_type=F32)
    inner = jnp.dot((scores * decay_ref[h]).astype(BF16), vh, preferred_element_type=F32)
    prev = state[h]
    cross = jnp.dot(qh, prev.astype(BF16), preferred_element_type=F32) * xi_ref[:, cols]
    local = lax.dot_general(kz_ref[rows, cols], vh, (((0,), (0,)), ((), ())),
                            preferred_element_type=F32)
    state[h] = prev * chunk_decay[h] + local
    o = inner + cross
    mu = jnp.mean(o, axis=-1, keepdims=True)
    oc = o - mu
    var = jnp.mean(oc * oc, axis=-1, keepdims=True)
    o = oc * lax.rsqrt(var + NORM_EPS) * gain_ref[:, cols]
    yb_ref[rows, cols] = (o * bz_ref[rows, cols].astype(F32)).astype(BF16)


def _retention_tables():
    log_g = np.log1p(-np.exp2(-5.0 - np.arange(RET_HEADS, dtype=np.float64)))
    idx = np.arange(RET_CHUNK, dtype=np.float64)
    diff = idx[:, None] - idx[None, :]
    decay = np.where(diff >= 0, np.exp(log_g[:, None, None] * np.maximum(diff, 0.0)), 0.0)
    xi = np.repeat(np.exp(log_g[None, :] * (idx[:, None] + 1.0)), RET_DK, axis=1)
    zeta = np.repeat(np.exp(log_g[None, :] * (RET_CHUNK - 1.0 - idx[:, None])), RET_DK, axis=1)
    zeta = np.tile(zeta, (PROJ_TILE // RET_CHUNK, 1))
    chunk_decay = tuple(float(x) for x in np.exp(log_g * RET_CHUNK))
    return (jnp.asarray(decay, F32), jnp.asarray(xi, F32), jnp.asarray(zeta, F32), chunk_decay)


def _even_out_kernel(x_ref, ys_ref, u_ref, az_ref, q_ref, k_ref, kz_ref, v_ref, bz_ref,
                     d_ref, wglu_ref, bglu_ref, w_ref, gain_ref, decay_ref, xi_ref,
                     o_ref, state, ybuf, *, chunk_decay, tiles_per_seq):
    @pl.when(pl.program_id(0) % tiles_per_seq == 0)
    def _():
        state[...] = jnp.zeros_like(state)

    units = [(ck, h) for ck in range(OUT_TILE // RET_CHUNK) for h in range(RET_HEADS)]

    def retention(selected):
        for ck, h in selected:
            _retention_unit(ck, h, q_ref, k_ref, kz_ref, v_ref, bz_ref, gain_ref, decay_ref,
                            xi_ref, state, ybuf, chunk_decay)

    ys = jnp.concatenate([ys_ref[s] for s in range(S5_SLABS)], axis=1)
    u = jnp.concatenate([u_ref[s] for s in range(S5_SLABS)], axis=1)
    y = ys.astype(F32) + d_ref[...] * u.astype(F32)
    y = jax.nn.gelu(y)
    gl = jnp.dot(y.astype(BF16), wglu_ref[...], preferred_element_type=F32) + bglu_ref[...]
    n_chunks = OUT_TILE // RET_CHUNK
    retention(units[:RET_HEADS])
    ya = (y * jax.nn.sigmoid(gl) * az_ref[...].astype(F32)).astype(BF16)
    acc = jnp.dot(ya, w_ref[0:S5_WIDTH, :], preferred_element_type=F32)
    for ck in range(n_chunks):
        rows = slice(ck * RET_CHUNK, (ck + 1) * RET_CHUNK)
        part = jnp.dot(ybuf[rows, :], w_ref[S5_WIDTH:, :], preferred_element_type=F32)
        if ck + 1 < n_chunks:
            retention(units[(ck + 1) * RET_HEADS:(ck + 2) * RET_HEADS])
        o_ref[rows, :] = x_ref[rows, :] + (acc[rows, :] + part)


def _even_out_proj(x2d, ys, u, az, q, k, kz, v, bz, d_skip, w_glu, b_glu, w_out, gain,
                   decay, xi, chunk_decay, seq):
    n = x2d.shape[0]
    tm = OUT_TILE
    assert tm % RET_CHUNK == 0 and seq % tm == 0
    tok = pl.BlockSpec((tm, D_MODEL), lambda i: (i, 0))
    slabs = pl.BlockSpec((S5_SLABS, tm, LANES), lambda i: (0, i, 0))
    return pl.pallas_call(
        functools.partial(_even_out_kernel, chunk_decay=chunk_decay, tiles_per_seq=seq // tm),
        grid=(n // tm,),
        in_specs=[tok, slabs, slabs, tok, tok, tok, tok, tok, tok,
                  _const_spec((1, S5_WIDTH)), _const_spec(w_glu.shape), _const_spec((1, S5_WIDTH)),
                  _const_spec(w_out.shape), _const_spec((1, D_MODEL)), _const_spec(decay.shape),
                  _const_spec(xi.shape)],
        out_specs=tok,
        out_shape=jax.ShapeDtypeStruct(x2d.shape, F32),
        scratch_shapes=[pltpu.VMEM((RET_HEADS, RET_DK, RET_DV), F32),
                        pltpu.VMEM((tm, D_MODEL), BF16)],
        compiler_params=pltpu.CompilerParams(
            dimension_semantics=("arbitrary",), vmem_limit_bytes=VMEM_LIMIT_BYTES),
        name="even_out_proj",
    )(x2d, ys, u, az, q, k, kz, v, bz, d_skip, w_glu, b_glu, w_out, gain, decay, xi)


def _odd_kernel(x_ref, g_ref, w_ref, vgain_ref, wsp_ref, bsp_ref, wout_ref, fg_ref, o_ref,
                vbuf, ubuf, gbuf, vnbuf, ybuf):
    x = x_ref[...]
    hn = _rms_norm(x, g_ref[...]).astype(BF16)
    gd = SGU_GROUP_DIM

    def proj(col):
        return jnp.dot(hn, w_ref[:, col:col + gd], preferred_element_type=F32)

    for g in range(SGU_GROUPS):
        vbuf[:, g * gd:(g + 1) * gd] = jax.nn.gelu(proj(SGU_WIDTH + g * gd))
    for g in range(SGU_GROUPS):
        ubuf[:, g * gd:(g + 1) * gd] = jax.nn.gelu(proj(g * gd))
        gbuf[:, g * gd:(g + 1) * gd] = _silu(proj(2 * SGU_WIDTH + g * gd))
    v = vbuf[...]
    mu = jnp.mean(v, axis=-1, keepdims=True)
    vc = v - mu
    var = jnp.mean(vc * vc, axis=-1, keepdims=True)
    vnbuf[...] = (vc * lax.rsqrt(var + NORM_EPS) * vgain_ref[...]).astype(BF16)

    row = lax.broadcasted_iota(jnp.int32, (SGU_CHUNK, SGU_CHUNK), 0)
    col = lax.broadcasted_iota(jnp.int32, (SGU_CHUNK, SGU_CHUNK), 1)
    for g in range(SGU_GROUPS):
        cols = slice(g * gd, (g + 1) * gd)
        wm = jnp.where(row >= col, wsp_ref[g], 0.0).astype(BF16)
        bias = bsp_ref[g]
        for c in range(ODD_TILE // SGU_CHUNK):
            rows = slice(c * SGU_CHUNK, (c + 1) * SGU_CHUNK)
            s = jnp.dot(wm, vnbuf[rows, cols], preferred_element_type=F32) + bias
            ybuf[rows, cols] = (ubuf[rows, cols] * s * gbuf[rows, cols]).astype(BF16)

    xn = x + jnp.dot(ybuf[...], wout_ref[...], preferred_element_type=F32)
    o_ref[...] = _rms_norm(xn, fg_ref[...])


def _odd_layer(x2d, gain, w_in, vgain, wsp, bsp, w_out, final_gain):
    n = x2d.shape[0]
    tm = ODD_TILE
    tok = pl.BlockSpec((tm, D_MODEL), lambda i: (i, 0))
    return pl.pallas_call(
        _odd_kernel,
        grid=(n // tm,),
        in_specs=[tok, _const_spec((1, D_MODEL)), _const_spec(w_in.shape),
                  _const_spec((1, SGU_WIDTH)), _const_spec(wsp.shape), _const_spec(bsp.shape),
                  _const_spec(w_out.shape), _const_spec((1, D_MODEL))],
        out_specs=tok,
        out_shape=jax.ShapeDtypeStruct(x2d.shape, F32),
        scratch_shapes=[pltpu.VMEM((tm, SGU_WIDTH), F32),
                        pltpu.VMEM((tm, SGU_WIDTH), F32),
                        pltpu.VMEM((tm, SGU_WIDTH), F32),
                        pltpu.VMEM((tm, SGU_WIDTH), BF16),
                        pltpu.VMEM((tm, SGU_WIDTH), BF16)],
        compiler_params=pltpu.CompilerParams(
            dimension_semantics=("arbitrary",), vmem_limit_bytes=VMEM_LIMIT_BYTES),
        name="odd_layer",
    )(x2d, gain, w_in, vgain, wsp, bsp, w_out, final_gain)


def _rotary_tables(seq):
    half = RET_DK // 2
    pos = np.arange(seq, dtype=np.float64)
    inv = ROPE_BASE ** (-np.arange(half, dtype=np.float64) / half)
    ang = pos[:, None] * inv[None, :]
    return jnp.asarray(np.cos(ang), F32), jnp.asarray(np.sin(ang), F32)


def kernel(x, norm_even, w_in_even, s5_lam_re, s5_lam_im, s5_log_dt, s5_b_re, s5_b_im, s5_c_re, s5_c_im, s5_d, s5_w_glu, s5_b_glu, ret_gn_gain, w_out_even, norm_odd, w_in_odd, sgu_norm_gain, sgu_w_spatial, sgu_b_spatial, w_out_odd, final_norm):
    bsz, seq, d = x.shape
    x2d = x.reshape(bsz * seq, d)
    cos, sin = _rotary_tables(seq)

    decay, xi, zeta, chunk_decay = _retention_tables()
    (u, az, q, k, kz, v, bz), (w_glu, w_out_e, w_in_o, w_out_o) = _even_in_proj(
        x2d, norm_even[0].reshape(1, d), w_in_even[0], cos, sin, zeta, seq,
        [s5_w_glu[0], w_out_even[0], w_in_odd[0], w_out_odd[0]])
    seg_len = seq // S5_BLOCK // S5_SEGS
    kw, cp, bpg, a_tab = _s5_params(s5_lam_re[0], s5_lam_im[0], s5_log_dt[0], s5_b_re[0],
                                    s5_b_im[0], s5_c_re[0], s5_c_im[0], seg_len)
    ys = _s5_branch(u, kw, cp, bpg, a_tab, bsz, seq)
    x2d = _even_out_proj(x2d, ys, u, az, q, k, kz, v, bz, s5_d[0].reshape(1, -1).astype(F32),
                         w_glu, s5_b_glu[0].reshape(1, -1).astype(F32), w_out_e,
                         ret_gn_gain[0].reshape(1, -1).astype(F32), decay, xi, chunk_decay, seq)

    out = _odd_layer(x2d, norm_odd[0].reshape(1, d), w_in_o,
                     sgu_norm_gain[0].reshape(1, -1).astype(F32), sgu_w_spatial[0].astype(F32),
                     sgu_b_spatial[0].astype(F32)[:, :, None], w_out_o,
                     final_norm.reshape(1, d))
    return out.reshape(bsz, seq, d)
```
